```python
import math
import jax, jax.numpy as jnp
from jax import lax
import numpy as np

D_MODEL = 1024
BATCH = 2
SEQ = 16384
DEPTH = 2

CHUNK = 64
N_MEM = 256
D_MIX = D_MODEL
POOL_WINDOWS = (2, 4, 8, 16)
POOL_GROUPS = len(POOL_WINDOWS)
POOL_CH = 64
W_A = POOL_GROUPS * POOL_CH
SG_BLOCK = 2 * CHUNK
SG_HEADS = 4
SG_HEAD_DIM = 96
W_B = SG_HEADS * SG_HEAD_DIM
CONV_K = 31
W_C = D_MIX - W_A - W_B
D_IN = W_A + 2 * W_B + 2 * W_C
X_HEADS = 4
X_HEAD_DIM = D_MODEL // X_HEADS
D_FF = 4 * D_MODEL
EPS = 1e-6

kernel_name = "hybrid_pool_sgmlp_conformer_block"


def rms_norm(x, g):
    xf = x.astype(jnp.float32)
    y = xf * lax.rsqrt(jnp.mean(jnp.square(xf), axis=-1, keepdims=True) + EPS)
    return (y * g.astype(jnp.float32)).astype(x.dtype)


def layer_norm(x, g, b):
    xf = x.astype(jnp.float32)
    mu = jnp.mean(xf, axis=-1, keepdims=True)
    var = jnp.mean(jnp.square(xf - mu), axis=-1, keepdims=True)
    y = (xf - mu) * lax.rsqrt(var + EPS)
    return (y * g.astype(jnp.float32) + b.astype(jnp.float32)).astype(x.dtype)


def pool_mixer(a, pool_w, pool_scale):
    bsz, s, _ = a.shape
    ag = a.reshape(bsz, s, POOL_GROUPS, POOL_CH).astype(jnp.float32)
    cs = jnp.cumsum(ag, axis=1)
    pos = jnp.arange(1, s + 1, dtype=jnp.float32)[None, :, None]
    outs = []
    for g, w in enumerate(POOL_WINDOWS):
        c = cs[:, :, g]
        lag = jnp.pad(c[:, : s - w], ((0, 0), (w, 0), (0, 0)))
        cnt = jnp.minimum(pos, float(w))
        outs.append((c - lag) / cnt - ag[:, :, g])
    p = jnp.stack(outs, axis=2).astype(a.dtype)
    y = jnp.einsum('bsgc,gcd->bsgd', p, pool_w)
    return y.reshape(bsz, s, W_A) * pool_scale


def spatial_gating_mixer(z, ln_g, ln_b, sg_w, sg_b):
    bsz, s, _ = z.shape
    z = jax.nn.gelu(z)
    u, v = jnp.split(z, 2, axis=-1)
    v = layer_norm(v, ln_g, ln_b)
    nb = s // SG_BLOCK
    v = v.reshape(bsz, nb, SG_BLOCK, SG_HEADS, SG_HEAD_DIM)
    u = u.reshape(bsz, nb, SG_BLOCK, SG_HEADS, SG_HEAD_DIM)
    mask = jnp.tril(jnp.ones((SG_BLOCK, SG_BLOCK), dtype=sg_w.dtype))
    w = sg_w * mask[None]
    sv = jnp.einsum('hts,bnshd->bnthd', w, v) + jnp.transpose(sg_b)[None, None, :, :, None]
    return (u * sv).reshape(bsz, s, W_B)


def conformer_conv_mixer(z, conv_w, conv_b, ln_g, ln_b):
    a, g = jnp.split(z, 2, axis=-1)
    h = a * jax.nn.sigmoid(g)
    h = lax.conv_general_dilated(
        h, conv_w[:, None, :], window_strides=(1,), padding=[(CONV_K - 1, 0)],
        dimension_numbers=('NWC', 'WIO', 'NWC'), feature_group_count=W_C)
    h = h + conv_b
    h = layer_norm(h, ln_g, ln_b)
    return jax.nn.silu(h)


def memory_cross_attention(h, m, wq, wk, wv, wo):
    bsz, s, _ = h.shape
    q = (h @ wq).reshape(bsz, s, X_HEADS, X_HEAD_DIM)
    k = (m @ wk).reshape(bsz, N_MEM, X_HEADS, X_HEAD_DIM)
    v = (m @ wv).reshape(bsz, N_MEM, X_HEADS, X_HEAD_DIM)
    sc = jnp.einsum('bshd,bmhd->bhsm', q, k).astype(jnp.float32) * (1.0 / math.sqrt(X_HEAD_DIM))
    p = jax.nn.softmax(sc, axis=-1).astype(v.dtype)
    o = jnp.einsum('bhsm,bmhd->bshd', p, v).reshape(bsz, s, D_MODEL)
    return o @ wo


def setup_inputs(seed: int = 0) -> dict:
    key = jax.random.key(seed)
    ks = iter(jax.random.split(key, 40))
    L, D = DEPTH, D_MODEL

    def nrm(shape, scale):
        return jax.random.normal(next(ks), shape, dtype=jnp.float32) * scale

    def gain(shape):
        return 1.0 + nrm(shape, 0.05)

    return {
        "x": nrm((BATCH, SEQ, D), 1.0),
        "mem": nrm((BATCH, N_MEM, D), 1.0),
        "pre_mix_g": gain((L, D)),
        "w_in": nrm((L, D, D_IN), D ** -0.5),
        "b_in": nrm((L, D_IN), 0.02),
        "pool_w": nrm((L, POOL_GROUPS, POOL_CH, POOL_CH), POOL_CH ** -0.5),
        "pool_scale": gain((L, W_A)),
        "sg_ln_g": gain((L, W_B)),
        "sg_ln_b": nrm((L, W_B), 0.02),
        "sg_w": nrm((L, SG_HEADS, SG_BLOCK, SG_BLOCK), SG_BLOCK ** -0.5),
        "sg_b": gain((L, SG_HEADS, SG_BLOCK)),
        "conv_w": nrm((L, CONV_K, W_C), CONV_K ** -0.5),
        "conv_b": nrm((L, W_C), 0.02),
        "conv_ln_g": gain((L, W_C)),
        "conv_ln_b": nrm((L, W_C), 0.02),
        "w_out": nrm((L, D_MIX, D), D_MIX ** -0.5),
        "post_mix_g": gain((L, D)),
        "pre_x_g": gain((L, D)),
        "mem_g": gain((L, D)),
        "wq": nrm((L, D, D), D ** -0.5),
        "wk": nrm((L, D, D), D ** -0.5),
        "wv": nrm((L, D, D), D ** -0.5),
        "wo": nrm((L, D, D), D ** -0.5),
        "post_x_g": gain((L, D)),
        "pre_ff_g": gain((L, D)),
        "w_ff1": nrm((L, D, D_FF), D ** -0.5),
        "w_ff2": nrm((L, D_FF, D), D_FF ** -0.5),
        "post_ff_g": gain((L, D)),
    }


def reference(x, mem, pre_mix_g, w_in, b_in, pool_w, pool_scale, sg_ln_g, sg_ln_b,
              sg_w, sg_b, conv_w, conv_b, conv_ln_g, conv_ln_b, w_out, post_mix_g,
              pre_x_g, mem_g, wq, wk, wv, wo, post_x_g, pre_ff_g, w_ff1, w_ff2,
              post_ff_g):
    for l in range(DEPTH):
        h = rms_norm(x, pre_mix_g[l])
        z = h @ w_in[l] + b_in[l]
        z_a = z[..., :W_A]
        z_b = z[..., W_A:W_A + 2 * W_B]
        z_c = z[..., W_A + 2 * W_B:]
        y_a = pool_mixer(z_a, pool_w[l], pool_scale[l])
        y_b = spatial_gating_mixer(z_b, sg_ln_g[l], sg_ln_b[l], sg_w[l], sg_b[l])
        y_c = conformer_conv_mixer(z_c, conv_w[l], conv_b[l], conv_ln_g[l], conv_ln_b[l])
        y = jnp.concatenate([y_a, y_b, y_c], axis=-1) @ w_out[l]
        x = x + rms_norm(y, post_mix_g[l])
        h = rms_norm(x, pre_x_g[l])
        m = rms_norm(mem, mem_g[l])
        y = memory_cross_attention(h, m, wq[l], wk[l], wv[l], wo[l])
        x = x + rms_norm(y, post_x_g[l])
        h = rms_norm(x, pre_ff_g[l])
        y = jnp.square(jax.nn.relu(h @ w_ff1[l])) @ w_ff2[l]
        x = x + rms_norm(y, post_ff_g[l])
    return x
```

```python
import functools
import math

import jax
import jax.numpy as jnp
from jax import lax
from jax.experimental import pallas as pl
from jax.experimental.pallas import tpu as pltpu

EPS = 1e-6
BF16 = jnp.bfloat16
F32 = jnp.float32

POOL_WINDOWS = (2, 4, 8, 16)
POOL_CH = 64
W_A = len(POOL_WINDOWS) * POOL_CH
SG_BLOCK = 128
SG_HEADS = 4
SG_HEAD_DIM = 96
W_B = SG_HEADS * SG_HEAD_DIM
CONV_K = 31
X_HEADS = 4

TM_MIX = 512
TM_ATT = 512
TM_FF = 512
FF_CHUNK = 1024
POOL_HALO = 16
CONV_HALO = 32
VMEM_LIMIT = 56 * 1024 * 1024


def _dot(a, b):
    return jnp.dot(a, b, preferred_element_type=F32)


def _rms(xf, g):
    ms = jnp.mean(xf * xf, axis=-1, keepdims=True)
    return xf * lax.rsqrt(ms + EPS) * g


def _layer_norm(xf, g, b):
    mu = jnp.mean(xf, axis=-1, keepdims=True)
    xc = xf - mu
    var = jnp.mean(xc * xc, axis=-1, keepdims=True)
    return xc * lax.rsqrt(var + EPS) * g + b


def _kv_kernel(mem_ref, g_ref, wq_ref, wk_ref, wv_ref, wo_ref, wqk_ref, vwo_ref):
    d_model = mem_ref.shape[-1]
    dh = d_model // X_HEADS
    scale = 1.0 / math.sqrt(dh)
    m = _rms(mem_ref[...], g_ref[...]).astype(BF16)
    k = _dot(m, wk_ref[...]).astype(BF16)
    v = _dot(m, wv_ref[...]).astype(BF16)
    for h in range(X_HEADS):
        hs = slice(h * dh, (h + 1) * dh)
        s = lax.dot_general(wq_ref[:, hs], k[:, hs], (((1,), (1,)), ((), ())),
                            preferred_element_type=F32)
        wqk_ref[:, hs] = (s * scale).astype(BF16)
        vwo_ref[hs, :] = _dot(v[:, hs], wo_ref[hs, :]).astype(BF16)


def _fold_memory(mem, mem_g, wq, wk, wv, wo):
    n_layers, d_model, _ = wq.shape
    bsz, n_mem, _ = mem.shape
    assert n_mem * X_HEADS == d_model
    w_spec = pl.BlockSpec((None, d_model, d_model), lambda l, b: (l, 0, 0))
    out_spec = pl.BlockSpec((None, None, d_model, d_model), lambda l, b: (l, b, 0, 0))
    out_shape = jax.ShapeDtypeStruct((n_layers, bsz, d_model, d_model), BF16)
    return pl.pallas_call(
        _kv_kernel,
        grid=(n_layers, bsz),
        in_specs=[
            pl.BlockSpec((None, n_mem, d_model), lambda l, b: (b, 0, 0)),
            pl.BlockSpec((None, 1, d_model), lambda l, b: (l, 0, 0)),
            w_spec, w_spec, w_spec, w_spec,
        ],
        out_specs=[out_spec, out_spec],
        out_shape=[out_shape, out_shape],
        compiler_params=pltpu.CompilerParams(
            dimension_semantics=("arbitrary", "arbitrary"),
            vmem_limit_bytes=VMEM_LIMIT),
        name="fold_memory",
    )(mem, mem_g, wq, wk, wv, wo)


def _mixer_kernel(x_ref, g1_ref, win_ref, bin_ref, poolw_ref, pscale_ref,
                  sglng_ref, sglnb_ref, sgw_ref, sgbias_ref,
                  convw_ref, convb_ref, clng_ref, clnb_ref, wout_ref, g2_ref,
                  o_ref, abuf, hbuf, ycat, hn_ref, zb_ref):
    tm = x_ref.shape[0]
    w_c = hbuf.shape[1]
    off_b = W_A
    off_c = W_A + 2 * W_B
    i = pl.program_id(1)

    @pl.when(i == 0)
    def _():
        abuf[0:POOL_HALO, :] = jnp.zeros((POOL_HALO, W_A), F32)
        hbuf[0:CONV_HALO, :] = jnp.zeros((CONV_HALO, w_c), F32)

    hn_ref[...] = _rms(x_ref[...], g1_ref[...]).astype(BF16)

    abuf[POOL_HALO:POOL_HALO + tm, :] = (
        _dot(hn_ref[...], win_ref[:, 0:W_A]) + bin_ref[:, 0:W_A])
    rows = 128
    lane = lax.broadcasted_iota(jnp.int32, (rows, 128), 1)
    row = lax.broadcasted_iota(jnp.int32, (rows, 128), 0)
    for r0 in range(0, tm, rows):
        pos1 = i * tm + r0 + row + 1
        cols = []
        for c, (w_lo, w_hi) in enumerate(((POOL_WINDOWS[0], POOL_WINDOWS[1]),
                                          (POOL_WINDOWS[2], POOL_WINDOWS[3]))):
            cs = slice(c * 128, (c + 1) * 128)
            base = POOL_HALO + r0
            a0 = abuf[base:base + rows, cs]
            s = a0
            for j in range(1, w_lo):
                s = s + abuf[base - j:base - j + rows, cs]
            s_lo = s
            for j in range(w_lo, w_hi):
                s = s + abuf[base - j:base - j + rows, cs]
            first = lane < POOL_CH
            win_sum = jnp.where(first, s_lo, s)
            cnt = jnp.minimum(pos1, jnp.where(first, w_lo, w_hi)).astype(F32)
            cols.append(win_sum / cnt - a0)
        p = jnp.concatenate(cols, axis=-1).astype(BF16)
        ya = _dot(p, poolw_ref[...]) * pscale_ref[...]
        ycat[r0:r0 + rows, 0:W_A] = ya.astype(BF16)
    abuf[0:POOL_HALO, :] = abuf[tm:tm + POOL_HALO, :]

    zb_ref[...] = jax.nn.gelu(
        _dot(hn_ref[...], win_ref[:, off_b:off_b + 2 * W_B]) + bin_ref[:, off_b:off_b + 2 * W_B])
    wrow = lax.broadcasted_iota(jnp.int32, (SG_HEADS * SG_BLOCK, SG_BLOCK), 0)
    wcol = lax.broadcasted_iota(jnp.int32, (SG_HEADS * SG_BLOCK, SG_BLOCK), 1)
    causal = (wrow & (SG_BLOCK - 1)) >= wcol
    wst = jnp.where(causal, sgw_ref[...], 0.0).astype(BF16)
    hl = lax.broadcasted_iota(jnp.int32, (SG_BLOCK, W_B), 1)
    for blk in range(tm // SG_BLOCK):
        rs = slice(blk * SG_BLOCK, (blk + 1) * SG_BLOCK)
        v = _layer_norm(zb_ref[rs, W_B:2 * W_B], sglng_ref[...], sglnb_ref[...]).astype(BF16)
        res = _dot(wst, v)
        sv = res[(SG_HEADS - 1) * SG_BLOCK:SG_HEADS * SG_BLOCK]
        for h in range(SG_HEADS - 2, -1, -1):
            sv = jnp.where(hl < (h + 1) * SG_HEAD_DIM, res[h * SG_BLOCK:(h + 1) * SG_BLOCK], sv)
        yb = zb_ref[rs, 0:W_B] * (sv + sgbias_ref[...])
        ycat[rs, W_A:W_A + W_B] = yb.astype(BF16)

    zb_ref[...] = _dot(hn_ref[...], win_ref[:, off_c:off_c + 2 * w_c]) + bin_ref[:, off_c:off_c + 2 * w_c]
    crows = 64
    for r0 in range(0, tm, crows):
        rs = slice(r0, r0 + crows)
        hbuf[CONV_HALO + r0:CONV_HALO + r0 + crows, :] = (
            zb_ref[rs, 0:w_c] * jax.nn.sigmoid(zb_ref[rs, w_c:2 * w_c]))
    lead = CONV_HALO - (CONV_K - 1)
    for r0 in range(0, tm, crows):
        acc = hbuf[r0 + lead:r0 + lead + crows, :] * convw_ref[0:1, :]
        for k in range(1, CONV_K):
            acc = acc + hbuf[r0 + lead + k:r0 + lead + k + crows, :] * convw_ref[k:k + 1, :]
        hc = _layer_norm(acc + convb_ref[...], clng_ref[...], clnb_ref[...])
        ycat[r0:r0 + crows, W_A + W_B:W_A + W_B + w_c] = jax.nn.silu(hc).astype(BF16)
    hbuf[0:CONV_HALO, :] = hbuf[tm:tm + CONV_HALO, :]

    y = _dot(ycat[...], wout_ref[...])
    o_ref[...] = x_ref[...] + _rms(y, g2_ref[...])


def _const_spec(shape, layer):
    nd = len(shape)
    return pl.BlockSpec((None,) + tuple(shape), lambda b, i: (layer,) + (0,) * nd,
                        pipeline_mode=pl.Buffered(1))


def _mixer(x, layer, p):
    bsz, seq, d_model = x.shape
    tm = TM_MIX
    assert seq % tm == 0 and tm % SG_BLOCK == 0
    d_in = p["w_in"].shape[-1]
    w_c = (d_in - W_A - 2 * W_B) // 2
    assert W_A + W_B + w_c == d_model
    x_spec = pl.BlockSpec((None, tm, d_model), lambda b, i: (b, i, 0))
    cs = functools.partial(_const_spec, layer=layer)
    return pl.pallas_call(
        _mixer_kernel,
        grid=(bsz, seq // tm),
        in_specs=[
            x_spec,
            cs((1, d_model)),
            cs((d_model, d_in)),
            cs((1, d_in)),
            cs((W_A, W_A)),
            cs((1, W_A)),
            cs((1, W_B)), cs((1, W_B)),
            cs((SG_HEADS * SG_BLOCK, SG_BLOCK)),
            cs((SG_BLOCK, W_B)),
            cs((CONV_K, w_c)), cs((1, w_c)),
            cs((1, w_c)), cs((1, w_c)),
            cs((d_model, d_model)),
            cs((1, d_model)),
        ],
        out_specs=x_spec,
        out_shape=jax.ShapeDtypeStruct(x.shape, x.dtype),
        scratch_shapes=[
            pltpu.VMEM((tm + POOL_HALO, W_A), F32),
            pltpu.VMEM((tm + CONV_HALO, w_c), F32),
            pltpu.VMEM((tm, d_model), BF16),
            pltpu.VMEM((tm, d_model), BF16),
            pltpu.VMEM((tm, 2 * W_B), F32),
        ],
        compiler_params=pltpu.CompilerParams(
            dimension_semantics=("arbitrary", "arbitrary"),
            vmem_limit_bytes=VMEM_LIMIT),
        name=f"mixer_l{layer}",
    )(x, p["pre_mix_g"], p["w_in"], p["b_in"], p["pool_w"], p["pool_scale"],
      p["sg_ln_g"], p["sg_ln_b"], p["sg_w"], p["sg_b"],
      p["conv_w"], p["conv_b"], p["conv_ln_g"], p["conv_ln_b"], p["w_out"], p["post_mix_g"])


def _attn_kernel(x_ref, g1_ref, wqk_ref, vwo_ref, g2_ref, o_ref, s_ref, p_ref):
    tm, d_model = x_ref.shape
    n_mem = d_model // X_HEADS
    hn = _rms(x_ref[...], g1_ref[...]).astype(BF16)
    s_ref[...] = _dot(hn, wqk_ref[...])
    rows = 128
    for r0 in range(0, tm, rows):
        rs = slice(r0, r0 + rows)
        for h in range(X_HEADS):
            hs = slice(h * n_mem, (h + 1) * n_mem)
            s = s_ref[rs, hs]
            e = jnp.exp(s - jnp.max(s, axis=-1, keepdims=True))
            p_ref[rs, hs] = (e / jnp.sum(e, axis=-1, keepdims=True)).astype(BF16)
    y = _dot(p_ref[...], vwo_ref[...])
    o_ref[...] = x_ref[...] + _rms(y, g2_ref[...])


def _attention(x, layer, p, wqk, vwo):
    bsz, seq, d_model = x.shape
    tm = TM_ATT
    assert seq % tm == 0
    x_spec = pl.BlockSpec((None, tm, d_model), lambda b, i: (b, i, 0))
    kv_spec = pl.BlockSpec((None, None, d_model, d_model), lambda b, i: (layer, b, 0, 0))
    cs = functools.partial(_const_spec, layer=layer)
    return pl.pallas_call(
        _attn_kernel,
        grid=(bsz, seq // tm),
        in_specs=[x_spec, cs((1, d_model)), kv_spec, kv_spec, cs((1, d_model))],
        out_specs=x_spec,
        out_shape=jax.ShapeDtypeStruct(x.shape, x.dtype),
        scratch_shapes=[pltpu.VMEM((tm, d_model), F32), pltpu.VMEM((tm, d_model), BF16)],
        compiler_params=pltpu.CompilerParams(
            dimension_semantics=("arbitrary", "arbitrary"),
            vmem_limit_bytes=VMEM_LIMIT),
        name=f"attn_l{layer}",
    )(x, p["pre_x_g"], wqk, vwo, p["post_x_g"])


def _ffn_kernel(x_ref, g1_ref, w1_ref, w2_ref, g2_ref, o_ref, hn_ref, hc_ref, acc_ref):
    d_ff = w1_ref.shape[1]
    hn_ref[...] = _rms(x_ref[...], g1_ref[...]).astype(BF16)
    for c in range(d_ff // FF_CHUNK):
        cs = slice(c * FF_CHUNK, (c + 1) * FF_CHUNK)
        a = jnp.maximum(_dot(hn_ref[...], w1_ref[:, cs]), 0.0)
        hc_ref[...] = (a * a).astype(BF16)
        part = _dot(hc_ref[...], w2_ref[cs, :])
        if c == 0:
            acc_ref[...] = part
        else:
            acc_ref[...] += part
    o_ref[...] = x_ref[...] + _rms(acc_ref[...], g2_ref[...])


def _ffn(x, layer, p):
    bsz, seq, d_model = x.shape
    d_ff = p["w_ff1"].shape[-1]
    tm = TM_FF
    assert seq % tm == 0 and d_ff % FF_CHUNK == 0
    x_spec = pl.BlockSpec((None, tm, d_model), lambda b, i: (b, i, 0))
    cs = functools.partial(_const_spec, layer=layer)
    return pl.pallas_call(
        _ffn_kernel,
        grid=(bsz, seq // tm),
        in_specs=[x_spec, cs((1, d_model)), cs((d_model, d_ff)), cs((d_ff, d_model)),
                  cs((1, d_model))],
        out_specs=x_spec,
        out_shape=jax.ShapeDtypeStruct(x.shape, x.dtype),
        scratch_shapes=[pltpu.VMEM((tm, d_model), BF16), pltpu.VMEM((tm, FF_CHUNK), BF16),
                        pltpu.VMEM((tm, d_model), F32)],
        compiler_params=pltpu.CompilerParams(
            dimension_semantics=("arbitrary", "arbitrary"),
            vmem_limit_bytes=VMEM_LIMIT),
        name=f"ffn_l{layer}",
    )(x, p["pre_ff_g"], p["w_ff1"], p["w_ff2"], p["post_ff_g"])


def kernel(x, mem, pre_mix_g, w_in, b_in, pool_w, pool_scale, sg_ln_g, sg_ln_b, sg_w, sg_b, conv_w, conv_b, conv_ln_g, conv_ln_b, w_out, post_mix_g, pre_x_g, mem_g, wq, wk, wv, wo, post_x_g, pre_ff_g, w_ff1, w_ff2, post_ff_g):
    n_layers = w_in.shape[0]
    n_groups = pool_w.shape[1]

    def row(v):
        return v[:, None, :]

    eye = jnp.eye(n_groups, dtype=pool_w.dtype)
    pool_bd = jnp.einsum("lgcd,gh->lgchd", pool_w, eye).reshape(n_layers, W_A, W_A)
    sg_bias = jnp.repeat(jnp.swapaxes(sg_b, 1, 2), SG_HEAD_DIM, axis=2)
    p = {
        "pre_mix_g": row(pre_mix_g), "w_in": w_in.astype(BF16), "b_in": row(b_in),
        "pool_w": pool_bd.astype(BF16), "pool_scale": row(pool_scale),
        "sg_ln_g": row(sg_ln_g), "sg_ln_b": row(sg_ln_b),
        "sg_w": sg_w.reshape(n_layers, SG_HEADS * SG_BLOCK, SG_BLOCK), "sg_b": sg_bias,
        "conv_w": conv_w, "conv_b": row(conv_b),
        "conv_ln_g": row(conv_ln_g), "conv_ln_b": row(conv_ln_b),
        "w_out": w_out.astype(BF16), "post_mix_g": row(post_mix_g),
        "pre_x_g": row(pre_x_g), "post_x_g": row(post_x_g),
        "pre_ff_g": row(pre_ff_g), "w_ff1": w_ff1.astype(BF16), "w_ff2": w_ff2.astype(BF16),
        "post_ff_g": row(post_ff_g),
    }
    wqk, vwo = _fold_memory(mem, row(mem_g), wq.astype(BF16), wk.astype(BF16),
                            wv.astype(BF16), wo.astype(BF16))
    for layer in range(n_layers):
        x = _mixer(x, layer, p)
        x = _attention(x, layer, p, wqk, vwo)
        x = _ffn(x, layer, p)
    return x
```

```python
import functools
import math

import jax
import jax.numpy as jnp
from jax import lax
from jax.experimental import pallas as pl
from jax.experimental.pallas import tpu as pltpu

EPS = 1e-6
BF16 = jnp.bfloat16
F32 = jnp.float32

POOL_WINDOWS = (2, 4, 8, 16)
POOL_CH = 64
W_A = len(POOL_WINDOWS) * POOL_CH
SG_BLOCK = 128
SG_HEADS = 4
SG_HEAD_DIM = 96
W_B = SG_HEADS * SG_HEAD_DIM
CONV_K = 31
X_HEADS = 4

TM_MIX = 512
TM_ATT = 512
TM_FF = 512
FF_CHUNK = 1024
POOL_HALO = 16
CONV_HALO = 32
VMEM_LIMIT = 56 * 1024 * 1024


def _dot(a, b):
    return jnp.dot(a, b, preferred_element_type=F32)


def _rms(xf, g):
    ms = jnp.mean(xf * xf, axis=-1, keepdims=True)
    return xf * lax.rsqrt(ms + EPS) * g


def _layer_norm(xf, g, b):
    mu = jnp.mean(xf, axis=-1, keepdims=True)
    xc = xf - mu
    var = jnp.mean(xc * xc, axis=-1, keepdims=True)
    return xc * lax.rsqrt(var + EPS) * g + b


def _kv_kernel(mem_ref, g_ref, wq_ref, wk_ref, wv_ref, wo_ref, wqk_ref, vwo_ref):
    d_model = mem_ref.shape[-1]
    dh = d_model // X_HEADS
    scale = 1.0 / math.sqrt(dh)
    m = _rms(mem_ref[...], g_ref[...]).astype(BF16)
    k = _dot(m, wk_ref[...]).astype(BF16)
    v = _dot(m, wv_ref[...]).astype(BF16)
    for h in range(X_HEADS):
        hs = slice(h * dh, (h + 1) * dh)
        s = lax.dot_general(wq_ref[:, hs], k[:, hs], (((1,), (1,)), ((), ())),
                            preferred_element_type=F32)
        wqk_ref[:, hs] = (s * scale).astype(BF16)
        vwo_ref[hs, :] = _dot(v[:, hs], wo_ref[hs, :]).astype(BF16)


def _fold_memory(mem, mem_g, wq, wk, wv, wo):
    n_layers, d_model, _ = wq.shape
    bsz, n_mem, _ = mem.shape
    assert n_mem * X_HEADS == d_model
    w_spec = pl.BlockSpec((None, d_model, d_model), lambda l, b: (l, 0, 0))
    out_spec = pl.BlockSpec((None, None, d_model, d_model), lambda l, b: (l, b, 0, 0))
    out_shape = jax.ShapeDtypeStruct((n_layers, bsz, d_model, d_model), BF16)
    return pl.pallas_call(
        _kv_kernel,
        grid=(n_layers, bsz),
        in_specs=[
            pl.BlockSpec((None, n_mem, d_model), lambda l, b: (b, 0, 0)),
            pl.BlockSpec((None, 1, d_model), lambda l, b: (l, 0, 0)),
            w_spec, w_spec, w_spec, w_spec,
        ],
        out_specs=[out_spec, out_spec],
        out_shape=[out_shape, out_shape],
        compiler_params=pltpu.CompilerParams(
            dimension_semantics=("arbitrary", "arbitrary"),
            vmem_limit_bytes=VMEM_LIMIT),
        name="fold_memory",
    )(mem, mem_g, wq, wk, wv, wo)


def _mixer_kernel(x_ref, g1_ref, win_ref, bin_ref, poolw_ref, pscale_ref,
                  sglng_ref, sglnb_ref, sgw_ref, sgbias_ref,
                  convw_ref, convb_ref, clng_ref, clnb_ref, wout_ref, g2_ref,
                  o_ref, abuf, hbuf, ycat, hn_ref, zb_ref):
    tm = x_ref.shape[0]
    w_c = hbuf.shape[1]
    off_b = W_A
    off_c = W_A + 2 * W_B
    i = pl.program_id(1)

    @pl.when(i == 0)
    def _():
        abuf[0:POOL_HALO, :] = jnp.zeros((POOL_HALO, W_A), F32)
        hbuf[0:CONV_HALO, :] = jnp.zeros((CONV_HALO, w_c), F32)

    hn_ref[...] = _rms(x_ref[...], g1_ref[...]).astype(BF16)

    abuf[POOL_HALO:POOL_HALO + tm, :] = (
        _dot(hn_ref[...], win_ref[:, 0:W_A]) + bin_ref[:, 0:W_A])
    rows = 128
    lane = lax.broadcasted_iota(jnp.int32, (rows, W_A), 1)
    row = lax.broadcasted_iota(jnp.int32, (rows, W_A), 0)
    group = lane // POOL_CH
    window = jnp.full((rows, W_A), POOL_WINDOWS[-1], jnp.int32)
    for gi in range(len(POOL_WINDOWS) - 2, -1, -1):
        window = jnp.where(group == gi, POOL_WINDOWS[gi], window)
    for r0 in range(0, tm, rows):
        s = abuf[r0:r0 + POOL_HALO + rows, :]
        a0 = s[POOL_HALO:]
        win_sum = None
        w = 1
        for gi, target in enumerate(POOL_WINDOWS):
            while w < target:
                s = s + pltpu.roll(s, w, 0)
                w *= 2
            cur = s[POOL_HALO:]
            win_sum = cur if win_sum is None else jnp.where(group >= gi, cur, win_sum)
        pos1 = i * tm + r0 + row + 1
        cnt = jnp.minimum(pos1, window).astype(F32)
        p = (win_sum / cnt - a0).astype(BF16)
        ya = _dot(p, poolw_ref[...]) * pscale_ref[...]
        ycat[r0:r0 + rows, 0:W_A] = ya.astype(BF16)
    abuf[0:POOL_HALO, :] = abuf[tm:tm + POOL_HALO, :]

    zb_ref[...] = jax.nn.gelu(
        _dot(hn_ref[...], win_ref[:, off_b:off_b + 2 * W_B]) + bin_ref[:, off_b:off_b + 2 * W_B])
    wrow = lax.broadcasted_iota(jnp.int32, (SG_HEADS * SG_BLOCK, SG_BLOCK), 0)
    wcol = lax.broadcasted_iota(jnp.int32, (SG_HEADS * SG_BLOCK, SG_BLOCK), 1)
    causal = (wrow & (SG_BLOCK - 1)) >= wcol
    wst = jnp.where(causal, sgw_ref[...], 0.0).astype(BF16)
    hl = lax.broadcasted_iota(jnp.int32, (SG_BLOCK, W_B), 1)
    for blk in range(tm // SG_BLOCK):
        rs = slice(blk * SG_BLOCK, (blk + 1) * SG_BLOCK)
        v = _layer_norm(zb_ref[rs, W_B:2 * W_B], sglng_ref[...], sglnb_ref[...]).astype(BF16)
        res = _dot(wst, v)
        sv = res[(SG_HEADS - 1) * SG_BLOCK:SG_HEADS * SG_BLOCK]
        for h in range(SG_HEADS - 2, -1, -1):
            sv = jnp.where(hl < (h + 1) * SG_HEAD_DIM, res[h * SG_BLOCK:(h + 1) * SG_BLOCK], sv)
        yb = zb_ref[rs, 0:W_B] * (sv + sgbias_ref[...])
        ycat[rs, W_A:W_A + W_B] = yb.astype(BF16)

    zb_ref[...] = _dot(hn_ref[...], win_ref[:, off_c:off_c + 2 * w_c]) + bin_ref[:, off_c:off_c + 2 * w_c]
    crows = 64
    for r0 in range(0, tm, crows):
        rs = slice(r0, r0 + crows)
        hbuf[CONV_HALO + r0:CONV_HALO + r0 + crows, :] = (
            zb_ref[rs, 0:w_c] * jax.nn.sigmoid(zb_ref[rs, w_c:2 * w_c]))
    lead = CONV_HALO - (CONV_K - 1)
    crows = 128
    for r0 in range(0, tm, crows):
        acc = None
        for j in range(8):
            taps = [o for o in range(j, lead + CONV_K, 8) if o >= lead]
            nrow = crows + 8 if j else crows
            q = None
            for o in taps:
                t = hbuf[r0 + o - j:r0 + o - j + nrow, :] * convw_ref[o - lead:o - lead + 1, :]
                q = t if q is None else q + t
            q = q[j:j + crows] if j else q
            acc = q if acc is None else acc + q
        hc = _layer_norm(acc + convb_ref[...], clng_ref[...], clnb_ref[...])
        ycat[r0:r0 + crows, W_A + W_B:W_A + W_B + w_c] = jax.nn.silu(hc).astype(BF16)
    hbuf[0:CONV_HALO, :] = hbuf[tm:tm + CONV_HALO, :]

    y = _dot(ycat[...], wout_ref[...])
    o_ref[...] = x_ref[...] + _rms(y, g2_ref[...])


def _const_spec(shape, layer):
    nd = len(shape)
    return pl.BlockSpec((None,) + tuple(shape), lambda b, i: (layer,) + (0,) * nd,
                        pipeline_mode=pl.Buffered(1))


def _mixer(x, layer, p):
    bsz, seq, d_model = x.shape
    tm = TM_MIX
    assert seq % tm == 0 and tm % SG_BLOCK == 0
    d_in = p["w_in"].shape[-1]
    w_c = (d_in - W_A - 2 * W_B) // 2
    assert W_A + W_B + w_c == d_model
    x_spec = pl.BlockSpec((None, tm, d_model), lambda b, i: (b, i, 0))
    cs = functools.partial(_const_spec, layer=layer)
    return pl.pallas_call(
        _mixer_kernel,
        grid=(bsz, seq // tm),
        in_specs=[
            x_spec,
            cs((1, d_model)),
            cs((d_model, d_in)),
            cs((1, d_in)),
            cs((W_A, W_A)),
            cs((1, W_A)),
            cs((1, W_B)), cs((1, W_B)),
            cs((SG_HEADS * SG_BLOCK, SG_BLOCK)),
            cs((SG_BLOCK, W_B)),
            cs((CONV_K, w_c)), cs((1, w_c)),
            cs((1, w_c)), cs((1, w_c)),
            cs((d_model, d_model)),
            cs((1, d_model)),
        ],
        out_specs=x_spec,
        out_shape=jax.ShapeDtypeStruct(x.shape, x.dtype),
        scratch_shapes=[
            pltpu.VMEM((tm + POOL_HALO, W_A), F32),
            pltpu.VMEM((tm + CONV_HALO, w_c), F32),
            pltpu.VMEM((tm, d_model), BF16),
            pltpu.VMEM((tm, d_model), BF16),
            pltpu.VMEM((tm, 2 * W_B), F32),
        ],
        compiler_params=pltpu.CompilerParams(
            dimension_semantics=("arbitrary", "arbitrary"),
            vmem_limit_bytes=VMEM_LIMIT),
        name=f"mixer_l{layer}",
    )(x, p["pre_mix_g"], p["w_in"], p["b_in"], p["pool_w"], p["pool_scale"],
      p["sg_ln_g"], p["sg_ln_b"], p["sg_w"], p["sg_b"],
      p["conv_w"], p["conv_b"], p["conv_ln_g"], p["conv_ln_b"], p["w_out"], p["post_mix_g"])


def _attn_kernel(x_ref, g1_ref, wqk_ref, vwo_ref, g2_ref, o_ref, s_ref, p_ref):
    tm, d_model = x_ref.shape
    n_mem = d_model // X_HEADS
    hn = _rms(x_ref[...], g1_ref[...]).astype(BF16)
    s_ref[...] = _dot(hn, wqk_ref[...])
    rows = 128
    for r0 in range(0, tm, rows):
        rs = slice(r0, r0 + rows)
        for h in range(X_HEADS):
            hs = slice(h * n_mem, (h + 1) * n_mem)
            s = s_ref[rs, hs]
            e = jnp.exp(s - jnp.max(s, axis=-1, keepdims=True))
            p_ref[rs, hs] = (e / jnp.sum(e, axis=-1, keepdims=True)).astype(BF16)
    y = _dot(p_ref[...], vwo_ref[...])
    o_ref[...] = x_ref[...] + _rms(y, g2_ref[...])


def _attention(x, layer, p, wqk, vwo):
    bsz, seq, d_model = x.shape
    tm = TM_ATT
    assert seq % tm == 0
    x_spec = pl.BlockSpec((None, tm, d_model), lambda b, i: (b, i, 0))
    kv_spec = pl.BlockSpec((None, None, d_model, d_model), lambda b, i: (layer, b, 0, 0))
    cs = functools.partial(_const_spec, layer=layer)
    return pl.pallas_call(
        _attn_kernel,
        grid=(bsz, seq // tm),
        in_specs=[x_spec, cs((1, d_model)), kv_spec, kv_spec, cs((1, d_model))],
        out_specs=x_spec,
        out_shape=jax.ShapeDtypeStruct(x.shape, x.dtype),
        scratch_shapes=[pltpu.VMEM((tm, d_model), F32), pltpu.VMEM((tm, d_model), BF16)],
        compiler_params=pltpu.CompilerParams(
            dimension_semantics=("arbitrary", "arbitrary"),
            vmem_limit_bytes=VMEM_LIMIT),
        name=f"attn_l{layer}",
    )(x, p["pre_x_g"], wqk, vwo, p["post_x_g"])


def _ffn_kernel(x_ref, g1_ref, w1_ref, w2_ref, g2_ref, o_ref, hn_ref, hc_ref, acc_ref):
    d_ff = w1_ref.shape[1]
    hn_ref[...] = _rms(x_ref[...], g1_ref[...]).astype(BF16)
    for c in range(d_ff // FF_CHUNK):
        cs = slice(c * FF_CHUNK, (c + 1) * FF_CHUNK)
        a = jnp.maximum(_dot(hn_ref[...], w1_ref[:, cs]), 0.0)
        hc_ref[...] = (a * a).astype(BF16)
        part = _dot(hc_ref[...], w2_ref[cs, :])
        if c == 0:
            acc_ref[...] = part
        else:
            acc_ref[...] += part
    o_ref[...] = x_ref[...] + _rms(acc_ref[...], g2_ref[...])


def _ffn(x, layer, p):
    bsz, seq, d_model = x.shape
    d_ff = p["w_ff1"].shape[-1]
    tm = TM_FF
    assert seq % tm == 0 and d_ff % FF_CHUNK == 0
    x_spec = pl.BlockSpec((None, tm, d_model), lambda b, i: (b, i, 0))
    cs = functools.partial(_const_spec, layer=layer)
    return pl.pallas_call(
        _ffn_kernel,
        grid=(bsz, seq // tm),
        in_specs=[x_spec, cs((1, d_model)), cs((d_model, d_ff)), cs((d_ff, d_model)),
                  cs((1, d_model))],
        out_specs=x_spec,
        out_shape=jax.ShapeDtypeStruct(x.shape, x.dtype),
        scratch_shapes=[pltpu.VMEM((tm, d_model), BF16), pltpu.VMEM((tm, FF_CHUNK), BF16),
                        pltpu.VMEM((tm, d_model), F32)],
        compiler_params=pltpu.CompilerParams(
            dimension_semantics=("arbitrary", "arbitrary"),
            vmem_limit_bytes=VMEM_LIMIT),
        name=f"ffn_l{layer}",
    )(x, p["pre_ff_g"], p["w_ff1"], p["w_ff2"], p["post_ff_g"])


def kernel(x, mem, pre_mix_g, w_in, b_in, pool_w, pool_scale, sg_ln_g, sg_ln_b, sg_w, sg_b, conv_w, conv_b, conv_ln_g, conv_ln_b, w_out, post_mix_g, pre_x_g, mem_g, wq, wk, wv, wo, post_x_g, pre_ff_g, w_ff1, w_ff2, post_ff_g):
    n_layers = w_in.shape[0]
    n_groups = pool_w.shape[1]

    def row(v):
        return v[:, None, :]

    eye = jnp.eye(n_groups, dtype=pool_w.dtype)
    pool_bd = jnp.einsum("lgcd,gh->lgchd", pool_w, eye).reshape(n_layers, W_A, W_A)
    sg_bias = jnp.repeat(jnp.swapaxes(sg_b, 1, 2), SG_HEAD_DIM, axis=2)
    p = {
        "pre_mix_g": row(pre_mix_g), "w_in": w_in.astype(BF16), "b_in": row(b_in),
        "pool_w": pool_bd.astype(BF16), "pool_scale": row(pool_scale),
        "sg_ln_g": row(sg_ln_g), "sg_ln_b": row(sg_ln_b),
        "sg_w": sg_w.reshape(n_layers, SG_HEADS * SG_BLOCK, SG_BLOCK), "sg_b": sg_bias,
        "conv_w": conv_w, "conv_b": row(conv_b),
        "conv_ln_g": row(conv_ln_g), "conv_ln_b": row(conv_ln_b),
        "w_out": w_out.astype(BF16), "post_mix_g": row(post_mix_g),
        "pre_x_g": row(pre_x_g), "post_x_g": row(post_x_g),
        "pre_ff_g": row(pre_ff_g), "w_ff1": w_ff1.astype(BF16), "w_ff2": w_ff2.astype(BF16),
        "post_ff_g": row(post_ff_g),
    }
    wqk, vwo = _fold_memory(mem, row(mem_g), wq.astype(BF16), wk.astype(BF16),
                            wv.astype(BF16), wo.astype(BF16))
    for layer in range(n_layers):
        x = _mixer(x, layer, p)
        x = _attention(x, layer, p, wqk, vwo)
        x = _ffn(x, layer, p)
    return x
```

```python
import functools
import math

import jax
import jax.numpy as jnp
from jax import lax
from jax.experimental import pallas as pl
from jax.experimental.pallas import tpu as pltpu

EPS = 1e-6
BF16 = jnp.bfloat16
F32 = jnp.float32

POOL_WINDOWS = (2, 4, 8, 16)
POOL_CH = 64
W_A = len(POOL_WINDOWS) * POOL_CH
SG_BLOCK = 128
SG_HEADS = 4
SG_HEAD_DIM = 96
W_B = SG_HEADS * SG_HEAD_DIM
CONV_K = 31
X_HEADS = 4

TM = 256
FF_CHUNK = 1024
SUBLANES = 8
POOL_HALO = 16
CONV_HALO = 32
VMEM_LIMIT = 56 * 1024 * 1024


def _dot(a, b):
    return jnp.dot(a, b, preferred_element_type=F32)


def _rms(xf, g):
    ms = jnp.mean(xf * xf, axis=-1, keepdims=True)
    return xf * lax.rsqrt(ms + EPS) * g


def _layer_norm(xf, g, b):
    mu = jnp.mean(xf, axis=-1, keepdims=True)
    xc = xf - mu
    var = jnp.mean(xc * xc, axis=-1, keepdims=True)
    return xc * lax.rsqrt(var + EPS) * g + b


def _kv_kernel(mem_ref, g_ref, wq_ref, wk_ref, wv_ref, wo_ref, wqk_ref, vwo_ref):
    d_model = mem_ref.shape[-1]
    dh = d_model // X_HEADS
    scale = 1.0 / math.sqrt(dh)
    m = _rms(mem_ref[...], g_ref[...]).astype(BF16)
    k = _dot(m, wk_ref[...]).astype(BF16)
    v = _dot(m, wv_ref[...]).astype(BF16)
    for h in range(X_HEADS):
        hs = slice(h * dh, (h + 1) * dh)
        s = lax.dot_general(wq_ref[:, hs], k[:, hs], (((1,), (1,)), ((), ())),
                            preferred_element_type=F32)
        wqk_ref[:, hs] = (s * scale).astype(BF16)
        vwo_ref[hs, :] = _dot(v[:, hs], wo_ref[hs, :]).astype(BF16)


def _fold_memory(mem, mem_g, wq, wk, wv, wo):
    n_layers, d_model, _ = wq.shape
    bsz, n_mem, _ = mem.shape
    assert n_mem * X_HEADS == d_model
    w_spec = pl.BlockSpec((None, d_model, d_model), lambda l, b: (l, 0, 0))
    out_spec = pl.BlockSpec((None, None, d_model, d_model), lambda l, b: (l, b, 0, 0))
    out_shape = jax.ShapeDtypeStruct((n_layers, bsz, d_model, d_model), BF16)
    return pl.pallas_call(
        _kv_kernel,
        grid=(n_layers, bsz),
        in_specs=[
            pl.BlockSpec((None, n_mem, d_model), lambda l, b: (b, 0, 0)),
            pl.BlockSpec((None, 1, d_model), lambda l, b: (l, 0, 0)),
            w_spec, w_spec, w_spec, w_spec,
        ],
        out_specs=[out_spec, out_spec],
        out_shape=[out_shape, out_shape],
        compiler_params=pltpu.CompilerParams(
            dimension_semantics=("arbitrary", "arbitrary"),
            vmem_limit_bytes=VMEM_LIMIT),
        name="fold_memory",
    )(mem, mem_g, wq, wk, wv, wo)


def _mixer_steps(x_ref, g1_ref, win_ref, bin_ref, poolw_ref, pscale_ref,
                 sglng_ref, sglnb_ref, sgw_ref, sgbias_ref,
                 convw_ref, convb_ref, clng_ref, clnb_ref, wout_ref, g2_ref,
                 o_ref, abuf, hbuf, ycat, hn_ref, zb_ref, pos0):
    tm = x_ref.shape[0]
    w_c = hbuf.shape[1]
    off_b = W_A
    off_c = W_A + 2 * W_B

    hn_ref[...] = _rms(x_ref[...], g1_ref[...]).astype(BF16)

    abuf[POOL_HALO:POOL_HALO + tm, :] = (
        _dot(hn_ref[...], win_ref[:, 0:W_A]) + bin_ref[:, 0:W_A])
    rows = 128
    lane = lax.broadcasted_iota(jnp.int32, (rows, W_A), 1)
    row = lax.broadcasted_iota(jnp.int32, (rows, W_A), 0)
    group = lane // POOL_CH
    window = jnp.full((rows, W_A), POOL_WINDOWS[-1], jnp.int32)
    for gi in range(len(POOL_WINDOWS) - 2, -1, -1):
        window = jnp.where(group == gi, POOL_WINDOWS[gi], window)
    for r0 in range(0, tm, rows):
        s = abuf[r0:r0 + POOL_HALO + rows, :]
        a0 = s[POOL_HALO:]
        win_sum = None
        w = 1
        for gi, target in enumerate(POOL_WINDOWS):
            while w < target:
                s = s + pltpu.roll(s, w, 0)
                w *= 2
            cur = s[POOL_HALO:]
            win_sum = cur if win_sum is None else jnp.where(group >= gi, cur, win_sum)
        pos1 = pos0 + r0 + row + 1
        cnt = jnp.minimum(pos1, window).astype(F32)
        p = (win_sum / cnt - a0).astype(BF16)
        ya = _dot(p, poolw_ref[...]) * pscale_ref[...]
        ycat[r0:r0 + rows, 0:W_A] = ya.astype(BF16)
    abuf[0:POOL_HALO, :] = abuf[tm:tm + POOL_HALO, :]
    yield

    zb_ref[...] = jax.nn.gelu(
        _dot(hn_ref[...], win_ref[:, off_b:off_b + 2 * W_B]) + bin_ref[:, off_b:off_b + 2 * W_B])
    yield
    wrow = lax.broadcasted_iota(jnp.int32, (SG_HEADS * SG_BLOCK, SG_BLOCK), 0)
    wcol = lax.broadcasted_iota(jnp.int32, (SG_HEADS * SG_BLOCK, SG_BLOCK), 1)
    causal = (wrow & (SG_BLOCK - 1)) >= wcol
    wst = jnp.where(causal, sgw_ref[...], 0.0).astype(BF16)
    hl = lax.broadcasted_iota(jnp.int32, (SG_BLOCK, W_B), 1)
    for blk in range(tm // SG_BLOCK):
        rs = slice(blk * SG_BLOCK, (blk + 1) * SG_BLOCK)
        v = _layer_norm(zb_ref[rs, W_B:2 * W_B], sglng_ref[...], sglnb_ref[...]).astype(BF16)
        res = _dot(wst, v)
        sv = res[(SG_HEADS - 1) * SG_BLOCK:SG_HEADS * SG_BLOCK]
        for h in range(SG_HEADS - 2, -1, -1):
            sv = jnp.where(hl < (h + 1) * SG_HEAD_DIM, res[h * SG_BLOCK:(h + 1) * SG_BLOCK], sv)
        yb = zb_ref[rs, 0:W_B] * (sv + sgbias_ref[...])
        ycat[rs, W_A:W_A + W_B] = yb.astype(BF16)
    yield

    zb_ref[...] = _dot(hn_ref[...], win_ref[:, off_c:off_c + 2 * w_c]) + bin_ref[:, off_c:off_c + 2 * w_c]
    grows = 64
    for r0 in range(0, tm, grows):
        rs = slice(r0, r0 + grows)
        hbuf[CONV_HALO + r0:CONV_HALO + r0 + grows, :] = (
            zb_ref[rs, 0:w_c] * jax.nn.sigmoid(zb_ref[rs, w_c:2 * w_c]))
    yield
    lead = CONV_HALO - (CONV_K - 1)
    crows = 128
    lanes = 128
    for r0 in range(0, tm, crows):
        for c0 in range(0, w_c, lanes):
            cs = slice(c0, c0 + lanes)
            acc = None
            for j in range(SUBLANES):
                taps = [o for o in range(j, lead + CONV_K, SUBLANES) if o >= lead]
                nrow = crows + SUBLANES if j else crows
                q = None
                for o in taps:
                    t = hbuf[r0 + o - j:r0 + o - j + nrow, cs] * convw_ref[o - lead:o - lead + 1, cs]
                    q = t if q is None else q + t
                q = q[j:j + crows] if j else q
                acc = q if acc is None else acc + q
            zb_ref[r0:r0 + crows, cs] = acc + convb_ref[:, cs]
        hc = _layer_norm(zb_ref[r0:r0 + crows, 0:w_c], clng_ref[...], clnb_ref[...])
        ycat[r0:r0 + crows, W_A + W_B:W_A + W_B + w_c] = jax.nn.silu(hc).astype(BF16)
        yield
    hbuf[0:CONV_HALO, :] = hbuf[tm:tm + CONV_HALO, :]

    y = _dot(ycat[...], wout_ref[...])
    o_ref[...] = x_ref[...] + _rms(y, g2_ref[...])
    yield


def _attn_steps(x_ref, g1_ref, wqk_ref, vwo_ref, g2_ref, o_ref, hn_ref, s_ref, p_ref):
    tm, d_model = x_ref.shape
    n_mem = d_model // X_HEADS
    o_ref[...] = x_ref[...]
    hn_ref[...] = _rms(x_ref[...], g1_ref[...]).astype(BF16)
    yield
    s_ref[...] = _dot(hn_ref[...], wqk_ref[...])
    rows = 128
    for r0 in range(0, tm, rows):
        rs = slice(r0, r0 + rows)
        for h in range(X_HEADS):
            hs = slice(h * n_mem, (h + 1) * n_mem)
            s = s_ref[rs, hs]
            e = jnp.exp(s - jnp.max(s, axis=-1, keepdims=True))
            p_ref[rs, hs] = (e / jnp.sum(e, axis=-1, keepdims=True)).astype(BF16)
    y = _dot(p_ref[...], vwo_ref[...])
    o_ref[...] += _rms(y, g2_ref[...])
    yield


def _ffn_steps(x_ref, g1_ref, w1_ref, w2_ref, g2_ref, o_ref, hn_ref, hc_ref, acc_ref):
    d_ff = w1_ref.shape[1]
    hn_ref[...] = _rms(x_ref[...], g1_ref[...]).astype(BF16)
    for c in range(d_ff // FF_CHUNK):
        cs = slice(c * FF_CHUNK, (c + 1) * FF_CHUNK)
        a = jnp.maximum(_dot(hn_ref[...], w1_ref[:, cs]), 0.0)
        hc_ref[...] = (a * a).astype(BF16)
        yield
        part = _dot(hc_ref[...], w2_ref[cs, :])
        if c == 0:
            acc_ref[...] = part
        else:
            acc_ref[...] += part
        yield
    o_ref[...] = x_ref[...] + _rms(acc_ref[...], g2_ref[...])
    yield


def _chain(*gens):
    for g in gens:
        yield from g


def _interleave(*gens):
    live = list(gens)
    while live:
        for g in list(live):
            try:
                next(g)
            except StopIteration:
                live.remove(g)


N_MIX_PARAMS = 15
N_ATT_PARAMS = 4
N_FFN_PARAMS = 4


def _layer_kernel(*refs, tiles_per_seq):
    x_ref = refs[0]
    mix_p = refs[1:1 + N_MIX_PARAMS]
    att_p = refs[1 + N_MIX_PARAMS:1 + N_MIX_PARAMS + N_ATT_PARAMS]
    ffn_p = refs[1 + N_MIX_PARAMS + N_ATT_PARAMS:1 + N_MIX_PARAMS + N_ATT_PARAMS + N_FFN_PARAMS]
    o_ref = refs[1 + N_MIX_PARAMS + N_ATT_PARAMS + N_FFN_PARAMS]
    (abuf, hbuf, ycat, hn_a, zb, x1, hn_b, s_ref, p_ref, x2, hn_c, hc, acc) = refs[-13:]
    tm = x_ref.shape[0]
    i = pl.program_id(0)
    n_tiles = pl.num_programs(0) - 1
    seq_tile = lax.rem(jnp.minimum(i, n_tiles - 1), tiles_per_seq)

    @pl.when(seq_tile == 0)
    def _():
        abuf[0:POOL_HALO, :] = jnp.zeros((POOL_HALO, abuf.shape[1]), F32)
        hbuf[0:CONV_HALO, :] = jnp.zeros((CONV_HALO, hbuf.shape[1]), F32)

    @pl.when(i == 0)
    def _():
        x1[...] = jnp.zeros(x1.shape, F32)

    attn = _attn_steps(x1, *att_p, x2, hn_b, s_ref, p_ref)
    next(attn)
    mixer = _mixer_steps(x_ref, *mix_p, x1, abuf, hbuf, ycat, hn_a, zb, seq_tile * tm)
    ffn = _ffn_steps(x2, *ffn_p, o_ref, hn_c, hc, acc)
    _interleave(mixer, _chain(attn, ffn))


def _layer(x, layer, p, wqk, vwo):
    bsz, seq, d_model = x.shape
    tm = TM
    assert seq % tm == 0 and tm % SG_BLOCK == 0
    tiles_per_seq = seq // tm
    n_tiles = bsz * tiles_per_seq
    d_in = p["w_in"].shape[-1]
    d_ff = p["w_ff1"].shape[-1]
    w_c = (d_in - W_A - 2 * W_B) // 2
    assert W_A + W_B + w_c == d_model and d_ff % FF_CHUNK == 0
    assert CONV_HALO % SUBLANES == 0 and CONV_HALO >= CONV_K - 1 and POOL_HALO >= POOL_WINDOWS[-1] - 1

    def in_tile(i):
        t = jnp.minimum(i, n_tiles - 1)
        return (t // tiles_per_seq, t % tiles_per_seq, 0)

    def out_tile(i):
        t = jnp.maximum(i - 1, 0)
        return (t // tiles_per_seq, t % tiles_per_seq, 0)

    def cs(*shape):
        nd = len(shape)
        return pl.BlockSpec((None,) + shape, lambda i: (layer,) + (0,) * nd,
                            pipeline_mode=pl.Buffered(1))

    kv_spec = pl.BlockSpec((None, None, d_model, d_model),
                           lambda i: (layer, jnp.maximum(i - 1, 0) // tiles_per_seq, 0, 0),
                           pipeline_mode=pl.Buffered(1))
    in_specs = [
        pl.BlockSpec((None, tm, d_model), in_tile),
        cs(1, d_model),
        cs(d_model, d_in),
        cs(1, d_in),
        cs(W_A, W_A),
        cs(1, W_A),
        cs(1, W_B), cs(1, W_B),
        cs(SG_HEADS * SG_BLOCK, SG_BLOCK),
        cs(SG_BLOCK, W_B),
        cs(CONV_K, w_c), cs(1, w_c),
        cs(1, w_c), cs(1, w_c),
        cs(d_model, d_model),
        cs(1, d_model),
        cs(1, d_model), kv_spec, kv_spec, cs(1, d_model),
        cs(1, d_model), cs(d_model, d_ff), cs(d_ff, d_model), cs(1, d_model),
    ]
    args = [x,
            p["pre_mix_g"], p["w_in"], p["b_in"], p["pool_w"], p["pool_scale"],
            p["sg_ln_g"], p["sg_ln_b"], p["sg_w"], p["sg_b"],
            p["conv_w"], p["conv_b"], p["conv_ln_g"], p["conv_ln_b"], p["w_out"], p["post_mix_g"],
            p["pre_x_g"], wqk, vwo, p["post_x_g"],
            p["pre_ff_g"], p["w_ff1"], p["w_ff2"], p["post_ff_g"]]
    assert len(args) == 1 + N_MIX_PARAMS + N_ATT_PARAMS + N_FFN_PARAMS == len(in_specs)
    scratch = [
        pltpu.VMEM((tm + POOL_HALO, W_A), F32),
        pltpu.VMEM((tm + CONV_HALO, w_c), F32),
        pltpu.VMEM((tm, d_model), BF16),
        pltpu.VMEM((tm, d_model), BF16),
        pltpu.VMEM((tm, 2 * W_B), F32),
        pltpu.VMEM((tm, d_model), F32),
        pltpu.VMEM((tm, d_model), BF16),
        pltpu.VMEM((tm, d_model), F32),
        pltpu.VMEM((tm, d_model), BF16),
        pltpu.VMEM((tm, d_model), F32),
        pltpu.VMEM((tm, d_model), BF16),
        pltpu.VMEM((tm, FF_CHUNK), BF16),
        pltpu.VMEM((tm, d_model), F32),
    ]
    return pl.pallas_call(
        functools.partial(_layer_kernel, tiles_per_seq=tiles_per_seq),
        grid=(n_tiles + 1,),
        in_specs=in_specs,
        out_specs=pl.BlockSpec((None, tm, d_model), out_tile),
        out_shape=jax.ShapeDtypeStruct(x.shape, x.dtype),
        scratch_shapes=scratch,
        compiler_params=pltpu.CompilerParams(
            dimension_semantics=("arbitrary",),
            vmem_limit_bytes=VMEM_LIMIT),
        name=f"layer{layer}",
    )(*args)


def kernel(x, mem, pre_mix_g, w_in, b_in, pool_w, pool_scale, sg_ln_g, sg_ln_b, sg_w, sg_b, conv_w, conv_b, conv_ln_g, conv_ln_b, w_out, post_mix_g, pre_x_g, mem_g, wq, wk, wv, wo, post_x_g, pre_ff_g, w_ff1, w_ff2, post_ff_g):
    n_layers = w_in.shape[0]
    n_groups = pool_w.shape[1]

    def row(v):
        return v[:, None, :]

    eye = jnp.eye(n_groups, dtype=pool_w.dtype)
    pool_bd = jnp.einsum("lgcd,gh->lgchd", pool_w, eye).reshape(n_layers, W_A, W_A)
    sg_bias = jnp.repeat(jnp.swapaxes(sg_b, 1, 2), SG_HEAD_DIM, axis=2)
    p = {
        "pre_mix_g": row(pre_mix_g), "w_in": w_in.astype(BF16), "b_in": row(b_in),
        "pool_w": pool_bd.astype(BF16), "pool_scale": row(pool_scale),
        "sg_ln_g": row(sg_ln_g), "sg_ln_b": row(sg_ln_b),
        "sg_w": sg_w.reshape(n_layers, SG_HEADS * SG_BLOCK, SG_BLOCK), "sg_b": sg_bias,
        "conv_w": conv_w, "conv_b": row(conv_b),
        "conv_ln_g": row(conv_ln_g), "conv_ln_b": row(conv_ln_b),
        "w_out": w_out.astype(BF16), "post_mix_g": row(post_mix_g),
        "pre_x_g": row(pre_x_g), "post_x_g": row(post_x_g),
        "pre_ff_g": row(pre_ff_g), "w_ff1": w_ff1.astype(BF16), "w_ff2": w_ff2.astype(BF16),
        "post_ff_g": row(post_ff_g),
    }
    wqk, vwo = _fold_memory(mem, row(mem_g), wq.astype(BF16), wk.astype(BF16),
                            wv.astype(BF16), wo.astype(BF16))
    for layer in range(n_layers):
        x = _layer(x, layer, p, wqk, vwo)
    return x
```

```python
import functools
import math

import jax
import jax.numpy as jnp
from jax import lax
from jax.experimental import pallas as pl
from jax.experimental.pallas import tpu as pltpu

EPS = 1e-6
BF16 = jnp.bfloat16
F32 = jnp.float32

POOL_WINDOWS = (2, 4, 8, 16)
POOL_CH = 64
W_A = len(POOL_WINDOWS) * POOL_CH
SG_BLOCK = 128
SG_HEADS = 4
SG_HEAD_DIM = 96
W_B = SG_HEADS * SG_HEAD_DIM
CONV_K = 31
X_HEADS = 4

TM_MIX = 512
TM_ATT = 512
TM_FF = 1024
LANES = 128
CONV_ROW_STRIDE = 2
FF_CHUNK = 1024
POOL_HALO = 16
CONV_HALO = 32
VMEM_LIMIT = 56 * 1024 * 1024


def _dot(a, b):
    return jnp.dot(a, b, preferred_element_type=F32)


def _rms(xf, g):
    ms = jnp.mean(xf * xf, axis=-1, keepdims=True)
    return xf * lax.rsqrt(ms + EPS) * g


def _layer_norm(xf, g, b):
    mu = jnp.mean(xf, axis=-1, keepdims=True)
    xc = xf - mu
    var = jnp.mean(xc * xc, axis=-1, keepdims=True)
    return xc * lax.rsqrt(var + EPS) * g + b


def _kv_kernel(mem_ref, g_ref, wq_ref, wk_ref, wv_ref, wo_ref, wqk_ref, vwo_ref):
    d_model = mem_ref.shape[-1]
    dh = d_model // X_HEADS
    scale = 1.0 / math.sqrt(dh)
    m = _rms(mem_ref[...], g_ref[...]).astype(BF16)
    k = _dot(m, wk_ref[...]).astype(BF16)
    v = _dot(m, wv_ref[...]).astype(BF16)
    for h in range(X_HEADS):
        hs = slice(h * dh, (h + 1) * dh)
        s = lax.dot_general(wq_ref[:, hs], k[:, hs], (((1,), (1,)), ((), ())),
                            preferred_element_type=F32)
        wqk_ref[:, hs] = (s * scale).astype(BF16)
        vwo_ref[hs, :] = _dot(v[:, hs], wo_ref[hs, :]).astype(BF16)


def _fold_memory(mem, mem_g, wq, wk, wv, wo):
    n_layers, d_model, _ = wq.shape
    bsz, n_mem, _ = mem.shape
    assert n_mem * X_HEADS == d_model
    w_spec = pl.BlockSpec((None, d_model, d_model), lambda l, b: (l, 0, 0))
    out_spec = pl.BlockSpec((None, None, d_model, d_model), lambda l, b: (l, b, 0, 0))
    out_shape = jax.ShapeDtypeStruct((n_layers, bsz, d_model, d_model), BF16)
    return pl.pallas_call(
        _kv_kernel,
        grid=(n_layers, bsz),
        in_specs=[
            pl.BlockSpec((None, n_mem, d_model), lambda l, b: (b, 0, 0)),
            pl.BlockSpec((None, 1, d_model), lambda l, b: (l, 0, 0)),
            w_spec, w_spec, w_spec, w_spec,
        ],
        out_specs=[out_spec, out_spec],
        out_shape=[out_shape, out_shape],
        compiler_params=pltpu.CompilerParams(
            dimension_semantics=("arbitrary", "arbitrary"),
            vmem_limit_bytes=VMEM_LIMIT),
        name="fold_memory",
    )(mem, mem_g, wq, wk, wv, wo)


def _mixer_kernel(x_ref, g1_ref, win_ref, bin_ref, poolw_ref, pscale_ref,
                  sglng_ref, sglnb_ref, sgw_ref, sgbias_ref,
                  convw_ref, convb_ref, clng_ref, clnb_ref, wout_ref, g2_ref,
                  o_ref, abuf, hbuf, cbuf, ycat, hn_ref, zb_ref):
    tm = x_ref.shape[0]
    n_slab = hbuf.shape[0]
    w_c = n_slab * LANES
    off_b = W_A
    off_c = W_A + 2 * W_B
    i = pl.program_id(1)

    @pl.when(i == 0)
    def _():
        abuf[0:POOL_HALO, :] = jnp.zeros((POOL_HALO, W_A), F32)
        hbuf[:, 0:CONV_HALO, :] = jnp.zeros((n_slab, CONV_HALO, LANES), F32)

    hn_ref[...] = _rms(x_ref[...], g1_ref[...]).astype(BF16)

    abuf[POOL_HALO:POOL_HALO + tm, :] = (
        _dot(hn_ref[...], win_ref[:, 0:W_A]) + bin_ref[:, 0:W_A])
    rows = 128
    lane = lax.broadcasted_iota(jnp.int32, (rows, W_A), 1)
    row = lax.broadcasted_iota(jnp.int32, (rows, W_A), 0)
    group = lane // POOL_CH
    window = jnp.full((rows, W_A), POOL_WINDOWS[-1], jnp.int32)
    for gi in range(len(POOL_WINDOWS) - 2, -1, -1):
        window = jnp.where(group == gi, POOL_WINDOWS[gi], window)
    for r0 in range(0, tm, rows):
        s = abuf[r0:r0 + POOL_HALO + rows, :]
        a0 = s[POOL_HALO:]
        win_sum = None
        w = 1
        for gi, target in enumerate(POOL_WINDOWS):
            while w < target:
                s = s + pltpu.roll(s, w, 0)
                w *= 2
            cur = s[POOL_HALO:]
            win_sum = cur if win_sum is None else jnp.where(group >= gi, cur, win_sum)
        pos1 = i * tm + r0 + row + 1
        cnt = jnp.minimum(pos1, window).astype(F32)
        p = (win_sum / cnt - a0).astype(BF16)
        ya = _dot(p, poolw_ref[...]) * pscale_ref[...]
        ycat[r0:r0 + rows, 0:W_A] = ya.astype(BF16)
    abuf[0:POOL_HALO, :] = abuf[tm:tm + POOL_HALO, :]

    zb_ref[...] = jax.nn.gelu(
        _dot(hn_ref[...], win_ref[:, off_b:off_b + 2 * W_B]) + bin_ref[:, off_b:off_b + 2 * W_B])
    wrow = lax.broadcasted_iota(jnp.int32, (SG_HEADS * SG_BLOCK, SG_BLOCK), 0)
    wcol = lax.broadcasted_iota(jnp.int32, (SG_HEADS * SG_BLOCK, SG_BLOCK), 1)
    causal = (wrow & (SG_BLOCK - 1)) >= wcol
    wst = jnp.where(causal, sgw_ref[...], 0.0).astype(BF16)
    hl = lax.broadcasted_iota(jnp.int32, (SG_BLOCK, W_B), 1)
    for blk in range(tm // SG_BLOCK):
        rs = slice(blk * SG_BLOCK, (blk + 1) * SG_BLOCK)
        v = _layer_norm(zb_ref[rs, W_B:2 * W_B], sglng_ref[...], sglnb_ref[...]).astype(BF16)
        res = _dot(wst, v)
        sv = res[(SG_HEADS - 1) * SG_BLOCK:SG_HEADS * SG_BLOCK]
        for h in range(SG_HEADS - 2, -1, -1):
            sv = jnp.where(hl < (h + 1) * SG_HEAD_DIM, res[h * SG_BLOCK:(h + 1) * SG_BLOCK], sv)
        yb = zb_ref[rs, 0:W_B] * (sv + sgbias_ref[...])
        ycat[rs, W_A:W_A + W_B] = yb.astype(BF16)

    zb_ref[...] = _dot(hn_ref[...], win_ref[:, off_c:off_c + 2 * w_c]) + bin_ref[:, off_c:off_c + 2 * w_c]
    grows = 64
    for r0 in range(0, tm, grows):
        rs = slice(r0, r0 + grows)
        glu = zb_ref[rs, 0:w_c] * jax.nn.sigmoid(zb_ref[rs, w_c:2 * w_c])
        for c in range(n_slab):
            hbuf[c, CONV_HALO + r0:CONV_HALO + r0 + grows, :] = glu[:, c * LANES:(c + 1) * LANES]
    lead = CONV_HALO - (CONV_K - 1)
    crows = 256
    nrow = crows // CONV_ROW_STRIDE
    for c in range(n_slab):
        cs = slice(c * LANES, (c + 1) * LANES)
        for r0 in range(0, tm, crows):
            for ph in range(CONV_ROW_STRIDE):
                acc = None
                for k in range(CONV_K):
                    t = (hbuf[c, pl.ds(r0 + ph + lead + k, nrow, stride=CONV_ROW_STRIDE), :]
                         * convw_ref[k:k + 1, cs])
                    acc = t if acc is None else acc + t
                cbuf[c, pl.ds(r0 + ph, nrow, stride=CONV_ROW_STRIDE), :] = acc + convb_ref[:, cs]
    lrows = 128
    for r0 in range(0, tm, lrows):
        conv = jnp.concatenate([cbuf[c, r0:r0 + lrows, :] for c in range(n_slab)], axis=-1)
        hc = _layer_norm(conv, clng_ref[...], clnb_ref[...])
        ycat[r0:r0 + lrows, W_A + W_B:W_A + W_B + w_c] = jax.nn.silu(hc).astype(BF16)
    hbuf[:, 0:CONV_HALO, :] = hbuf[:, tm:tm + CONV_HALO, :]

    y = _dot(ycat[...], wout_ref[...])
    o_ref[...] = x_ref[...] + _rms(y, g2_ref[...])


def _const_spec(shape, layer):
    nd = len(shape)
    return pl.BlockSpec((None,) + tuple(shape), lambda b, i: (layer,) + (0,) * nd,
                        pipeline_mode=pl.Buffered(1))


def _mixer(x, layer, p):
    bsz, seq, d_model = x.shape
    tm = TM_MIX
    assert seq % tm == 0 and tm % SG_BLOCK == 0
    d_in = p["w_in"].shape[-1]
    w_c = (d_in - W_A - 2 * W_B) // 2
    assert W_A + W_B + w_c == d_model and w_c % LANES == 0
    assert CONV_HALO >= CONV_K - 1 and tm % 256 == 0
    x_spec = pl.BlockSpec((None, tm, d_model), lambda b, i: (b, i, 0))
    cs = functools.partial(_const_spec, layer=layer)
    return pl.pallas_call(
        _mixer_kernel,
        grid=(bsz, seq // tm),
        in_specs=[
            x_spec,
            cs((1, d_model)),
            cs((d_model, d_in)),
            cs((1, d_in)),
            cs((W_A, W_A)),
            cs((1, W_A)),
            cs((1, W_B)), cs((1, W_B)),
            cs((SG_HEADS * SG_BLOCK, SG_BLOCK)),
            cs((SG_BLOCK, W_B)),
            cs((CONV_K, w_c)), cs((1, w_c)),
            cs((1, w_c)), cs((1, w_c)),
            cs((d_model, d_model)),
            cs((1, d_model)),
        ],
        out_specs=x_spec,
        out_shape=jax.ShapeDtypeStruct(x.shape, x.dtype),
        scratch_shapes=[
            pltpu.VMEM((tm + POOL_HALO, W_A), F32),
            pltpu.VMEM((w_c // LANES, tm + CONV_HALO, LANES), F32),
            pltpu.VMEM((w_c // LANES, tm, LANES), F32),
            pltpu.VMEM((tm, d_model), BF16),
            pltpu.VMEM((tm, d_model), BF16),
            pltpu.VMEM((tm, 2 * W_B), F32),
        ],
        compiler_params=pltpu.CompilerParams(
            dimension_semantics=("arbitrary", "arbitrary"),
            vmem_limit_bytes=VMEM_LIMIT),
        name=f"mixer_l{layer}",
    )(x, p["pre_mix_g"], p["w_in"], p["b_in"], p["pool_w"], p["pool_scale"],
      p["sg_ln_g"], p["sg_ln_b"], p["sg_w"], p["sg_b"],
      p["conv_w"], p["conv_b"], p["conv_ln_g"], p["conv_ln_b"], p["w_out"], p["post_mix_g"])


def _attn_kernel(x_ref, g1_ref, wqk_ref, vwo_ref, g2_ref, o_ref, s_ref, p_ref):
    tm, d_model = x_ref.shape
    n_mem = d_model // X_HEADS
    hn = _rms(x_ref[...], g1_ref[...]).astype(BF16)
    s_ref[...] = _dot(hn, wqk_ref[...])
    rows = 128
    for r0 in range(0, tm, rows):
        rs = slice(r0, r0 + rows)
        for h in range(X_HEADS):
            hs = slice(h * n_mem, (h + 1) * n_mem)
            s = s_ref[rs, hs]
            e = jnp.exp(s - jnp.max(s, axis=-1, keepdims=True))
            p_ref[rs, hs] = (e / jnp.sum(e, axis=-1, keepdims=True)).astype(BF16)
    y = _dot(p_ref[...], vwo_ref[...])
    o_ref[...] = x_ref[...] + _rms(y, g2_ref[...])


def _attention(x, layer, p, wqk, vwo):
    bsz, seq, d_model = x.shape
    tm = TM_ATT
    assert seq % tm == 0
    x_spec = pl.BlockSpec((None, tm, d_model), lambda b, i: (b, i, 0))
    kv_spec = pl.BlockSpec((None, None, d_model, d_model), lambda b, i: (layer, b, 0, 0))
    cs = functools.partial(_const_spec, layer=layer)
    return pl.pallas_call(
        _attn_kernel,
        grid=(bsz, seq // tm),
        in_specs=[x_spec, cs((1, d_model)), kv_spec, kv_spec, cs((1, d_model))],
        out_specs=x_spec,
        out_shape=jax.ShapeDtypeStruct(x.shape, x.dtype),
        scratch_shapes=[pltpu.VMEM((tm, d_model), F32), pltpu.VMEM((tm, d_model), BF16)],
        compiler_params=pltpu.CompilerParams(
            dimension_semantics=("arbitrary", "arbitrary"),
            vmem_limit_bytes=VMEM_LIMIT),
        name=f"attn_l{layer}",
    )(x, p["pre_x_g"], wqk, vwo, p["post_x_g"])


def _ffn_kernel(x_ref, g1_ref, w1_ref, w2_ref, g2_ref, o_ref, hn_ref, hc_ref, acc_ref):
    d_ff = w1_ref.shape[1]
    hn_ref[...] = _rms(x_ref[...], g1_ref[...]).astype(BF16)
    for c in range(d_ff // FF_CHUNK):
        cs = slice(c * FF_CHUNK, (c + 1) * FF_CHUNK)
        a = jnp.maximum(_dot(hn_ref[...], w1_ref[:, cs]), 0.0)
        hc_ref[...] = (a * a).astype(BF16)
        part = _dot(hc_ref[...], w2_ref[cs, :])
        if c == 0:
            acc_ref[...] = part
        else:
            acc_ref[...] += part
    o_ref[...] = x_ref[...] + _rms(acc_ref[...], g2_ref[...])


def _ffn(x, layer, p):
    bsz, seq, d_model = x.shape
    d_ff = p["w_ff1"].shape[-1]
    tm = TM_FF
    assert seq % tm == 0 and d_ff % FF_CHUNK == 0
    x_spec = pl.BlockSpec((None, tm, d_model), lambda b, i: (b, i, 0))
    cs = functools.partial(_const_spec, layer=layer)
    return pl.pallas_call(
        _ffn_kernel,
        grid=(bsz, seq // tm),
        in_specs=[x_spec, cs((1, d_model)), cs((d_model, d_ff)), cs((d_ff, d_model)),
                  cs((1, d_model))],
        out_specs=x_spec,
        out_shape=jax.ShapeDtypeStruct(x.shape, x.dtype),
        scratch_shapes=[pltpu.VMEM((tm, d_model), BF16), pltpu.VMEM((tm, FF_CHUNK), BF16),
                        pltpu.VMEM((tm, d_model), F32)],
        compiler_params=pltpu.CompilerParams(
            dimension_semantics=("arbitrary", "arbitrary"),
            vmem_limit_bytes=VMEM_LIMIT),
        name=f"ffn_l{layer}",
    )(x, p["pre_ff_g"], p["w_ff1"], p["w_ff2"], p["post_ff_g"])


def kernel(x, mem, pre_mix_g, w_in, b_in, pool_w, pool_scale, sg_ln_g, sg_ln_b, sg_w, sg_b, conv_w, conv_b, conv_ln_g, conv_ln_b, w_out, post_mix_g, pre_x_g, mem_g, wq, wk, wv, wo, post_x_g, pre_ff_g, w_ff1, w_ff2, post_ff_g):
    n_layers = w_in.shape[0]
    n_groups = pool_w.shape[1]

    def row(v):
        return v[:, None, :]

    eye = jnp.eye(n_groups, dtype=pool_w.dtype)
    pool_bd = jnp.einsum("lgcd,gh->lgchd", pool_w, eye).reshape(n_layers, W_A, W_A)
    sg_bias = jnp.repeat(jnp.swapaxes(sg_b, 1, 2), SG_HEAD_DIM, axis=2)
    p = {
        "pre_mix_g": row(pre_mix_g), "w_in": w_in.astype(BF16), "b_in": row(b_in),
        "pool_w": pool_bd.astype(BF16), "pool_scale": row(pool_scale),
        "sg_ln_g": row(sg_ln_g), "sg_ln_b": row(sg_ln_b),
        "sg_w": sg_w.reshape(n_layers, SG_HEADS * SG_BLOCK, SG_BLOCK), "sg_b": sg_bias,
        "conv_w": conv_w, "conv_b": row(conv_b),
        "conv_ln_g": row(conv_ln_g), "conv_ln_b": row(conv_ln_b),
        "w_out": w_out.astype(BF16), "post_mix_g": row(post_mix_g),
        "pre_x_g": row(pre_x_g), "post_x_g": row(post_x_g),
        "pre_ff_g": row(pre_ff_g), "w_ff1": w_ff1.astype(BF16), "w_ff2": w_ff2.astype(BF16),
        "post_ff_g": row(post_ff_g),
    }
    wqk, vwo = _fold_memory(mem, row(mem_g), wq.astype(BF16), wk.astype(BF16),
                            wv.astype(BF16), wo.astype(BF16))
    for layer in range(n_layers):
        x = _mixer(x, layer, p)
        x = _attention(x, layer, p, wqk, vwo)
        x = _ffn(x, layer, p)
    return x
```

```python
import functools
import math

import jax
import jax.numpy as jnp
from jax import lax
from jax.experimental import pallas as pl
from jax.experimental.pallas import tpu as pltpu

EPS = 1e-6
BF16 = jnp.bfloat16
F32 = jnp.float32

POOL_WINDOWS = (2, 4, 8, 16)
POOL_CH = 64
W_A = len(POOL_WINDOWS) * POOL_CH
SG_BLOCK = 128
SG_HEADS = 4
SG_HEAD_DIM = 96
W_B = SG_HEADS * SG_HEAD_DIM
CONV_K = 31
X_HEADS = 4

TM_MIX = 512
TM_ATT = 512
TM_FF = 1024
LANES = 128
CONV_ROW_STRIDE = 2
FF_CHUNK = 1024
POOL_HALO = 16
CONV_HALO = 32
VMEM_LIMIT = 56 * 1024 * 1024


def _dot(a, b):
    return jnp.dot(a, b, preferred_element_type=F32)


def _rms(xf, g):
    ms = jnp.mean(xf * xf, axis=-1, keepdims=True)
    return xf * lax.rsqrt(ms + EPS) * g


def _layer_norm(xf, g, b):
    mu = jnp.mean(xf, axis=-1, keepdims=True)
    xc = xf - mu
    var = jnp.mean(xc * xc, axis=-1, keepdims=True)
    return xc * lax.rsqrt(var + EPS) * g + b


def _kv_kernel(mem_ref, g_ref, wq_ref, wk_ref, wv_ref, wo_ref, wqk_ref, vwo_ref):
    d_model = mem_ref.shape[-1]
    dh = d_model // X_HEADS
    scale = 1.0 / math.sqrt(dh)
    m = _rms(mem_ref[...], g_ref[...]).astype(BF16)
    k = _dot(m, wk_ref[...]).astype(BF16)
    v = _dot(m, wv_ref[...]).astype(BF16)
    for h in range(X_HEADS):
        hs = slice(h * dh, (h + 1) * dh)
        s = lax.dot_general(wq_ref[:, hs], k[:, hs], (((1,), (1,)), ((), ())),
                            preferred_element_type=F32)
        wqk_ref[:, hs] = (s * scale).astype(BF16)
        vwo_ref[hs, :] = _dot(v[:, hs], wo_ref[hs, :]).astype(BF16)


def _fold_memory(mem, mem_g, wq, wk, wv, wo):
    n_layers, d_model, _ = wq.shape
    bsz, n_mem, _ = mem.shape
    assert n_mem * X_HEADS == d_model
    w_spec = pl.BlockSpec((None, d_model, d_model), lambda l, b: (l, 0, 0))
    out_spec = pl.BlockSpec((None, None, d_model, d_model), lambda l, b: (l, b, 0, 0))
    out_shape = jax.ShapeDtypeStruct((n_layers, bsz, d_model, d_model), BF16)
    return pl.pallas_call(
        _kv_kernel,
        grid=(n_layers, bsz),
        in_specs=[
            pl.BlockSpec((None, n_mem, d_model), lambda l, b: (b, 0, 0)),
            pl.BlockSpec((None, 1, d_model), lambda l, b: (l, 0, 0)),
            w_spec, w_spec, w_spec, w_spec,
        ],
        out_specs=[out_spec, out_spec],
        out_shape=[out_shape, out_shape],
        compiler_params=pltpu.CompilerParams(
            dimension_semantics=("arbitrary", "arbitrary"),
            vmem_limit_bytes=VMEM_LIMIT),
        name="fold_memory",
    )(mem, mem_g, wq, wk, wv, wo)


def _mixer_kernel(x_ref, g1_ref, win_ref, bin_ref, poolw_ref, pscale_ref,
                  sglng_ref, sglnb_ref, sgw_ref, sgbias_ref,
                  convw_ref, convb_ref, clng_ref, clnb_ref, wout_ref, g2_ref,
                  o_ref, abuf, hbuf, cbuf, ycat, hn_ref, z_ref):
    tm = x_ref.shape[0]
    n_slab = hbuf.shape[0]
    w_c = n_slab * LANES
    off_b = W_A
    off_c = W_A + 2 * W_B
    i = pl.program_id(1)

    @pl.when(i == 0)
    def _():
        abuf[0:POOL_HALO, :] = jnp.zeros((POOL_HALO, W_A), F32)
        hbuf[:, 0:CONV_HALO, :] = jnp.zeros((n_slab, CONV_HALO, LANES), F32)

    hn_ref[...] = _rms(x_ref[...], g1_ref[...]).astype(BF16)
    z_ref[...] = _dot(hn_ref[...], win_ref[...]) + bin_ref[...]

    abuf[POOL_HALO:POOL_HALO + tm, :] = z_ref[:, 0:W_A]
    rows = 128
    lane = lax.broadcasted_iota(jnp.int32, (rows, W_A), 1)
    row = lax.broadcasted_iota(jnp.int32, (rows, W_A), 0)
    group = lane // POOL_CH
    window = jnp.full((rows, W_A), POOL_WINDOWS[-1], jnp.int32)
    for gi in range(len(POOL_WINDOWS) - 2, -1, -1):
        window = jnp.where(group == gi, POOL_WINDOWS[gi], window)
    for r0 in range(0, tm, rows):
        s = abuf[r0:r0 + POOL_HALO + rows, :]
        a0 = s[POOL_HALO:]
        win_sum = None
        w = 1
        for gi, target in enumerate(POOL_WINDOWS):
            while w < target:
                s = s + pltpu.roll(s, w, 0)
                w *= 2
            cur = s[POOL_HALO:]
            win_sum = cur if win_sum is None else jnp.where(group >= gi, cur, win_sum)
        pos1 = i * tm + r0 + row + 1
        cnt = jnp.minimum(pos1, window).astype(F32)
        p = (win_sum / cnt - a0).astype(BF16)
        ya = _dot(p, poolw_ref[...]) * pscale_ref[...]
        ycat[r0:r0 + rows, 0:W_A] = ya.astype(BF16)
    abuf[0:POOL_HALO, :] = abuf[tm:tm + POOL_HALO, :]

    wrow = lax.broadcasted_iota(jnp.int32, (SG_HEADS * SG_BLOCK, SG_BLOCK), 0)
    wcol = lax.broadcasted_iota(jnp.int32, (SG_HEADS * SG_BLOCK, SG_BLOCK), 1)
    causal = (wrow & (SG_BLOCK - 1)) >= wcol
    wst = jnp.where(causal, sgw_ref[...], 0.0).astype(BF16)
    hl = lax.broadcasted_iota(jnp.int32, (SG_BLOCK, W_B), 1)
    for blk in range(tm // SG_BLOCK):
        rs = slice(blk * SG_BLOCK, (blk + 1) * SG_BLOCK)
        v = _layer_norm(jax.nn.gelu(z_ref[rs, off_b + W_B:off_b + 2 * W_B]),
                        sglng_ref[...], sglnb_ref[...]).astype(BF16)
        res = _dot(wst, v)
        sv = res[(SG_HEADS - 1) * SG_BLOCK:SG_HEADS * SG_BLOCK]
        for h in range(SG_HEADS - 2, -1, -1):
            sv = jnp.where(hl < (h + 1) * SG_HEAD_DIM, res[h * SG_BLOCK:(h + 1) * SG_BLOCK], sv)
        yb = jax.nn.gelu(z_ref[rs, off_b:off_b + W_B]) * (sv + sgbias_ref[...])
        ycat[rs, W_A:W_A + W_B] = yb.astype(BF16)

    grows = 64
    for r0 in range(0, tm, grows):
        rs = slice(r0, r0 + grows)
        glu = z_ref[rs, off_c:off_c + w_c] * jax.nn.sigmoid(z_ref[rs, off_c + w_c:off_c + 2 * w_c])
        for c in range(n_slab):
            hbuf[c, CONV_HALO + r0:CONV_HALO + r0 + grows, :] = glu[:, c * LANES:(c + 1) * LANES]
    lead = CONV_HALO - (CONV_K - 1)
    crows = 256
    nrow = crows // CONV_ROW_STRIDE
    for c in range(n_slab):
        cs = slice(c * LANES, (c + 1) * LANES)
        for r0 in range(0, tm, crows):
            for ph in range(CONV_ROW_STRIDE):
                acc = None
                for k in range(CONV_K):
                    t = (hbuf[c, pl.ds(r0 + ph + lead + k, nrow, stride=CONV_ROW_STRIDE), :]
                         * convw_ref[k:k + 1, cs])
                    acc = t if acc is None else acc + t
                cbuf[c, pl.ds(r0 + ph, nrow, stride=CONV_ROW_STRIDE), :] = acc + convb_ref[:, cs]
    lrows = 128
    for r0 in range(0, tm, lrows):
        conv = jnp.concatenate([cbuf[c, r0:r0 + lrows, :] for c in range(n_slab)], axis=-1)
        hc = _layer_norm(conv, clng_ref[...], clnb_ref[...])
        ycat[r0:r0 + lrows, W_A + W_B:W_A + W_B + w_c] = jax.nn.silu(hc).astype(BF16)
    hbuf[:, 0:CONV_HALO, :] = hbuf[:, tm:tm + CONV_HALO, :]

    y = _dot(ycat[...], wout_ref[...])
    o_ref[...] = x_ref[...] + _rms(y, g2_ref[...])


def _const_spec(shape, layer):
    nd = len(shape)
    return pl.BlockSpec((None,) + tuple(shape), lambda b, i: (layer,) + (0,) * nd,
                        pipeline_mode=pl.Buffered(1))


def _mixer(x, layer, p):
    bsz, seq, d_model = x.shape
    tm = TM_MIX
    assert seq % tm == 0 and tm % SG_BLOCK == 0
    d_in = p["w_in"].shape[-1]
    w_c = (d_in - W_A - 2 * W_B) // 2
    assert W_A + W_B + w_c == d_model and w_c % LANES == 0
    assert CONV_HALO >= CONV_K - 1 and tm % 256 == 0
    x_spec = pl.BlockSpec((None, tm, d_model), lambda b, i: (b, i, 0))
    cs = functools.partial(_const_spec, layer=layer)
    return pl.pallas_call(
        _mixer_kernel,
        grid=(bsz, seq // tm),
        in_specs=[
            x_spec,
            cs((1, d_model)),
            cs((d_model, d_in)),
            cs((1, d_in)),
            cs((W_A, W_A)),
            cs((1, W_A)),
            cs((1, W_B)), cs((1, W_B)),
            cs((SG_HEADS * SG_BLOCK, SG_BLOCK)),
            cs((SG_BLOCK, W_B)),
            cs((CONV_K, w_c)), cs((1, w_c)),
            cs((1, w_c)), cs((1, w_c)),
            cs((d_model, d_model)),
            cs((1, d_model)),
        ],
        out_specs=x_spec,
        out_shape=jax.ShapeDtypeStruct(x.shape, x.dtype),
        scratch_shapes=[
            pltpu.VMEM((tm + POOL_HALO, W_A), F32),
            pltpu.VMEM((w_c // LANES, tm + CONV_HALO, LANES), F32),
            pltpu.VMEM((w_c // LANES, tm, LANES), F32),
            pltpu.VMEM((tm, d_model), BF16),
            pltpu.VMEM((tm, d_model), BF16),
            pltpu.VMEM((tm, d_in), F32),
        ],
        compiler_params=pltpu.CompilerParams(
            dimension_semantics=("arbitrary", "arbitrary"),
            vmem_limit_bytes=VMEM_LIMIT),
        name=f"mixer_l{layer}",
    )(x, p["pre_mix_g"], p["w_in"], p["b_in"], p["pool_w"], p["pool_scale"],
      p["sg_ln_g"], p["sg_ln_b"], p["sg_w"], p["sg_b"],
      p["conv_w"], p["conv_b"], p["conv_ln_g"], p["conv_ln_b"], p["w_out"], p["post_mix_g"])


def _attn_kernel(x_ref, g1_ref, wqk_ref, vwo_ref, g2_ref, o_ref, hn_ref, s_ref, p_ref, acc_ref):
    tm, d_model = x_ref.shape
    n_mem = d_model // X_HEADS
    hn_ref[...] = _rms(x_ref[...], g1_ref[...]).astype(BF16)
    s_ref[...] = _dot(hn_ref[...], wqk_ref[...])
    rows = 128
    for h in range(X_HEADS):
        hs = slice(h * n_mem, (h + 1) * n_mem)
        for r0 in range(0, tm, rows):
            s = s_ref[r0:r0 + rows, hs]
            e = jnp.exp(s - jnp.max(s, axis=-1, keepdims=True))
            p_ref[r0:r0 + rows, :] = (e / jnp.sum(e, axis=-1, keepdims=True)).astype(BF16)
        part = _dot(p_ref[...], vwo_ref[hs, :])
        if h == 0:
            acc_ref[...] = part
        else:
            acc_ref[...] += part
    o_ref[...] = x_ref[...] + _rms(acc_ref[...], g2_ref[...])


def _attention(x, layer, p, wqk, vwo):
    bsz, seq, d_model = x.shape
    tm = TM_ATT
    assert seq % tm == 0
    x_spec = pl.BlockSpec((None, tm, d_model), lambda b, i: (b, i, 0))
    kv_spec = pl.BlockSpec((None, None, d_model, d_model), lambda b, i: (layer, b, 0, 0))
    cs = functools.partial(_const_spec, layer=layer)
    return pl.pallas_call(
        _attn_kernel,
        grid=(bsz, seq // tm),
        in_specs=[x_spec, cs((1, d_model)), kv_spec, kv_spec, cs((1, d_model))],
        out_specs=x_spec,
        out_shape=jax.ShapeDtypeStruct(x.shape, x.dtype),
        scratch_shapes=[pltpu.VMEM((tm, d_model), BF16),
                        pltpu.VMEM((tm, d_model), F32),
                        pltpu.VMEM((tm, d_model // X_HEADS), BF16),
                        pltpu.VMEM((tm, d_model), F32)],
        compiler_params=pltpu.CompilerParams(
            dimension_semantics=("arbitrary", "arbitrary"),
            vmem_limit_bytes=VMEM_LIMIT),
        name=f"attn_l{layer}",
    )(x, p["pre_x_g"], wqk, vwo, p["post_x_g"])


def _ffn_kernel(x_ref, g1_ref, w1_ref, w2_ref, g2_ref, o_ref, hn_ref, hc_ref, acc_ref):
    d_ff = w1_ref.shape[1]
    hn_ref[...] = _rms(x_ref[...], g1_ref[...]).astype(BF16)
    for c in range(d_ff // FF_CHUNK):
        cs = slice(c * FF_CHUNK, (c + 1) * FF_CHUNK)
        a = jnp.maximum(_dot(hn_ref[...], w1_ref[:, cs]), 0.0)
        hc_ref[...] = (a * a).astype(BF16)
        part = _dot(hc_ref[...], w2_ref[cs, :])
        if c == 0:
            acc_ref[...] = part
        else:
            acc_ref[...] += part
    o_ref[...] = x_ref[...] + _rms(acc_ref[...], g2_ref[...])


def _ffn(x, layer, p):
    bsz, seq, d_model = x.shape
    d_ff = p["w_ff1"].shape[-1]
    tm = TM_FF
    assert seq % tm == 0 and d_ff % FF_CHUNK == 0
    x_spec = pl.BlockSpec((None, tm, d_model), lambda b, i: (b, i, 0))
    cs = functools.partial(_const_spec, layer=layer)
    return pl.pallas_call(
        _ffn_kernel,
        grid=(bsz, seq // tm),
        in_specs=[x_spec, cs((1, d_model)), cs((d_model, d_ff)), cs((d_ff, d_model)),
                  cs((1, d_model))],
        out_specs=x_spec,
        out_shape=jax.ShapeDtypeStruct(x.shape, x.dtype),
        scratch_shapes=[pltpu.VMEM((tm, d_model), BF16), pltpu.VMEM((tm, FF_CHUNK), BF16),
                        pltpu.VMEM((tm, d_model), F32)],
        compiler_params=pltpu.CompilerParams(
            dimension_semantics=("arbitrary", "arbitrary"),
            vmem_limit_bytes=VMEM_LIMIT),
        name=f"ffn_l{layer}",
    )(x, p["pre_ff_g"], p["w_ff1"], p["w_ff2"], p["post_ff_g"])


def kernel(x, mem, pre_mix_g, w_in, b_in, pool_w, pool_scale, sg_ln_g, sg_ln_b, sg_w, sg_b, conv_w, conv_b, conv_ln_g, conv_ln_b, w_out, post_mix_g, pre_x_g, mem_g, wq, wk, wv, wo, post_x_g, pre_ff_g, w_ff1, w_ff2, post_ff_g):
    n_layers = w_in.shape[0]
    n_groups = pool_w.shape[1]

    def row(v):
        return v[:, None, :]

    eye = jnp.eye(n_groups, dtype=pool_w.dtype)
    pool_bd = jnp.einsum("lgcd,gh->lgchd", pool_w, eye).reshape(n_layers, W_A, W_A)
    sg_bias = jnp.repeat(jnp.swapaxes(sg_b, 1, 2), SG_HEAD_DIM, axis=2)
    p = {
        "pre_mix_g": row(pre_mix_g), "w_in": w_in.astype(BF16), "b_in": row(b_in),
        "pool_w": pool_bd.astype(BF16), "pool_scale": row(pool_scale),
        "sg_ln_g": row(sg_ln_g), "sg_ln_b": row(sg_ln_b),
        "sg_w": sg_w.reshape(n_layers, SG_HEADS * SG_BLOCK, SG_BLOCK), "sg_b": sg_bias,
        "conv_w": conv_w, "conv_b": row(conv_b),
        "conv_ln_g": row(conv_ln_g), "conv_ln_b": row(conv_ln_b),
        "w_out": w_out.astype(BF16), "post_mix_g": row(post_mix_g),
        "pre_x_g": row(pre_x_g), "post_x_g": row(post_x_g),
        "pre_ff_g": row(pre_ff_g), "w_ff1": w_ff1.astype(BF16), "w_ff2": w_ff2.astype(BF16),
        "post_ff_g": row(post_ff_g),
    }
    wqk, vwo = _fold_memory(mem, row(mem_g), wq.astype(BF16), wk.astype(BF16),
                            wv.astype(BF16), wo.astype(BF16))
    for layer in range(n_layers):
        x = _mixer(x, layer, p)
        x = _attention(x, layer, p, wqk, vwo)
        x = _ffn(x, layer, p)
    return x
```

```python
import functools
import math

import jax
import jax.numpy as jnp
from jax import lax
from jax.experimental import pallas as pl
from jax.experimental.pallas import tpu as pltpu

EPS = 1e-6
BF16 = jnp.bfloat16
F32 = jnp.float32

POOL_WINDOWS = (2, 4, 8, 16)
POOL_CH = 64
W_A = len(POOL_WINDOWS) * POOL_CH
SG_BLOCK = 128
SG_HEADS = 4
SG_HEAD_DIM = 96
W_B = SG_HEADS * SG_HEAD_DIM
CONV_K = 31
X_HEADS = 4

TM_MIX = 512
TM_ATT = 512
TM_FF = 512
LANES = 128
CONV_ROW_STRIDE = 2
FF_CHUNK = 1024
POOL_HALO = 16
CONV_HALO = 32
VMEM_LIMIT = 56 * 1024 * 1024


def _dot(a, b):
    return jnp.dot(a, b, preferred_element_type=F32)


def _rms(xf, g):
    ms = jnp.mean(xf * xf, axis=-1, keepdims=True)
    return xf * lax.rsqrt(ms + EPS) * g


def _layer_norm(xf, g, b):
    mu = jnp.mean(xf, axis=-1, keepdims=True)
    xc = xf - mu
    var = jnp.mean(xc * xc, axis=-1, keepdims=True)
    return xc * lax.rsqrt(var + EPS) * g + b


def _kv_kernel(mem_ref, g_ref, wq_ref, wk_ref, wv_ref, wo_ref, wqk_ref, vwo_ref):
    d_model = mem_ref.shape[-1]
    dh = d_model // X_HEADS
    scale = 1.0 / math.sqrt(dh)
    m = _rms(mem_ref[...], g_ref[...]).astype(BF16)
    k = _dot(m, wk_ref[...].astype(BF16)).astype(BF16)
    v = _dot(m, wv_ref[...].astype(BF16)).astype(BF16)
    for h in range(X_HEADS):
        hs = slice(h * dh, (h + 1) * dh)
        s = lax.dot_general(wq_ref[:, hs].astype(BF16), k[:, hs], (((1,), (1,)), ((), ())),
                            preferred_element_type=F32)
        wqk_ref[:, hs] = (s * scale).astype(BF16)
        vwo_ref[hs, :] = _dot(v[:, hs], wo_ref[hs, :].astype(BF16)).astype(BF16)


def _fold_memory(mem, mem_g, wq, wk, wv, wo):
    n_layers, d_model, _ = wq.shape
    bsz, n_mem, _ = mem.shape
    assert n_mem * X_HEADS == d_model
    w_spec = pl.BlockSpec((None, d_model, d_model), lambda l, b: (l, 0, 0))
    out_spec = pl.BlockSpec((None, None, d_model, d_model), lambda l, b: (l, b, 0, 0))
    out_shape = jax.ShapeDtypeStruct((n_layers, bsz, d_model, d_model), BF16)
    return pl.pallas_call(
        _kv_kernel,
        grid=(n_layers, bsz),
        in_specs=[
            pl.BlockSpec((None, n_mem, d_model), lambda l, b: (b, 0, 0)),
            pl.BlockSpec((None, 1, d_model), lambda l, b: (l, 0, 0)),
            w_spec, w_spec, w_spec, w_spec,
        ],
        out_specs=[out_spec, out_spec],
        out_shape=[out_shape, out_shape],
        compiler_params=pltpu.CompilerParams(
            dimension_semantics=("arbitrary", "arbitrary"),
            vmem_limit_bytes=VMEM_LIMIT),
        name="fold_memory",
    )(mem, mem_g, wq, wk, wv, wo)


def _mixer_kernel(x_ref, g1_ref, win_ref, bin_ref, poolw_ref, pscale_ref,
                  sglng_ref, sglnb_ref, sgw_ref, sgbias_ref,
                  convw_ref, convb_ref, clng_ref, clnb_ref, wout_ref, g2_ref,
                  o_ref, abuf, hbuf, cbuf, ycat, hn_ref, z_ref):
    tm = x_ref.shape[0]
    n_slab = hbuf.shape[0]
    w_c = n_slab * LANES
    off_b = W_A
    off_c = W_A + 2 * W_B
    i = pl.program_id(1)

    @pl.when(i == 0)
    def _():
        abuf[0:POOL_HALO, :] = jnp.zeros((POOL_HALO, W_A), F32)
        hbuf[:, 0:CONV_HALO, :] = jnp.zeros((n_slab, CONV_HALO, LANES), F32)

    hn_ref[...] = _rms(x_ref[...], g1_ref[...]).astype(BF16)
    z_ref[...] = _dot(hn_ref[...], win_ref[...].astype(BF16)) + bin_ref[...]

    abuf[POOL_HALO:POOL_HALO + tm, :] = z_ref[:, 0:W_A]
    rows = 128
    lane = lax.broadcasted_iota(jnp.int32, (rows, W_A), 1)
    row = lax.broadcasted_iota(jnp.int32, (rows, W_A), 0)
    group = lane // POOL_CH
    window = jnp.full((rows, W_A), POOL_WINDOWS[-1], jnp.int32)
    for gi in range(len(POOL_WINDOWS) - 2, -1, -1):
        window = jnp.where(group == gi, POOL_WINDOWS[gi], window)
    for r0 in range(0, tm, rows):
        s = abuf[r0:r0 + POOL_HALO + rows, :]
        a0 = s[POOL_HALO:]
        win_sum = None
        w = 1
        for gi, target in enumerate(POOL_WINDOWS):
            while w < target:
                s = s + pltpu.roll(s, w, 0)
                w *= 2
            cur = s[POOL_HALO:]
            win_sum = cur if win_sum is None else jnp.where(group >= gi, cur, win_sum)
        pos1 = i * tm + r0 + row + 1
        cnt = jnp.minimum(pos1, window).astype(F32)
        p = (win_sum / cnt - a0).astype(BF16)
        ya = _dot(p, poolw_ref[...]) * pscale_ref[...]
        ycat[r0:r0 + rows, 0:W_A] = ya.astype(BF16)
    abuf[0:POOL_HALO, :] = abuf[tm:tm + POOL_HALO, :]

    wrow = lax.broadcasted_iota(jnp.int32, (SG_HEADS * SG_BLOCK, SG_BLOCK), 0)
    wcol = lax.broadcasted_iota(jnp.int32, (SG_HEADS * SG_BLOCK, SG_BLOCK), 1)
    causal = (wrow & (SG_BLOCK - 1)) >= wcol
    wst = jnp.where(causal, sgw_ref[...], 0.0).astype(BF16)
    hl = lax.broadcasted_iota(jnp.int32, (SG_BLOCK, W_B), 1)
    for blk in range(tm // SG_BLOCK):
        rs = slice(blk * SG_BLOCK, (blk + 1) * SG_BLOCK)
        v = _layer_norm(jax.nn.gelu(z_ref[rs, off_b + W_B:off_b + 2 * W_B]),
                        sglng_ref[...], sglnb_ref[...]).astype(BF16)
        res = _dot(wst, v)
        sv = res[(SG_HEADS - 1) * SG_BLOCK:SG_HEADS * SG_BLOCK]
        for h in range(SG_HEADS - 2, -1, -1):
            sv = jnp.where(hl < (h + 1) * SG_HEAD_DIM, res[h * SG_BLOCK:(h + 1) * SG_BLOCK], sv)
        yb = jax.nn.gelu(z_ref[rs, off_b:off_b + W_B]) * (sv + sgbias_ref[...])
        ycat[rs, W_A:W_A + W_B] = yb.astype(BF16)

    grows = 64
    for r0 in range(0, tm, grows):
        rs = slice(r0, r0 + grows)
        glu = z_ref[rs, off_c:off_c + w_c] * jax.nn.sigmoid(z_ref[rs, off_c + w_c:off_c + 2 * w_c])
        for c in range(n_slab):
            hbuf[c, CONV_HALO + r0:CONV_HALO + r0 + grows, :] = glu[:, c * LANES:(c + 1) * LANES]
    lead = CONV_HALO - (CONV_K - 1)
    crows = 256
    nrow = crows // CONV_ROW_STRIDE
    for c in range(n_slab):
        cs = slice(c * LANES, (c + 1) * LANES)
        for r0 in range(0, tm, crows):
            for ph in range(CONV_ROW_STRIDE):
                acc = None
                for k in range(CONV_K):
                    t = (hbuf[c, pl.ds(r0 + ph + lead + k, nrow, stride=CONV_ROW_STRIDE), :]
                         * convw_ref[k:k + 1, cs])
                    acc = t if acc is None else acc + t
                cbuf[c, pl.ds(r0 + ph, nrow, stride=CONV_ROW_STRIDE), :] = acc + convb_ref[:, cs]
    lrows = 128
    for r0 in range(0, tm, lrows):
        conv = jnp.concatenate([cbuf[c, r0:r0 + lrows, :] for c in range(n_slab)], axis=-1)
        hc = _layer_norm(conv, clng_ref[...], clnb_ref[...])
        ycat[r0:r0 + lrows, W_A + W_B:W_A + W_B + w_c] = jax.nn.silu(hc).astype(BF16)
    hbuf[:, 0:CONV_HALO, :] = hbuf[:, tm:tm + CONV_HALO, :]

    y = _dot(ycat[...], wout_ref[...].astype(BF16))
    o_ref[...] = x_ref[...] + _rms(y, g2_ref[...])


def _const_spec(shape, layer):
    nd = len(shape)
    return pl.BlockSpec((None,) + tuple(shape), lambda b, i: (layer,) + (0,) * nd,
                        pipeline_mode=pl.Buffered(1))


def _mixer(x, layer, p):
    bsz, seq, d_model = x.shape
    tm = TM_MIX
    assert seq % tm == 0 and tm % SG_BLOCK == 0
    d_in = p["w_in"].shape[-1]
    w_c = (d_in - W_A - 2 * W_B) // 2
    assert W_A + W_B + w_c == d_model and w_c % LANES == 0
    assert CONV_HALO >= CONV_K - 1 and tm % 256 == 0
    x_spec = pl.BlockSpec((None, tm, d_model), lambda b, i: (b, i, 0))
    cs = functools.partial(_const_spec, layer=layer)
    return pl.pallas_call(
        _mixer_kernel,
        grid=(bsz, seq // tm),
        in_specs=[
            x_spec,
            cs((1, d_model)),
            cs((d_model, d_in)),
            cs((1, d_in)),
            cs((W_A, W_A)),
            cs((1, W_A)),
            cs((1, W_B)), cs((1, W_B)),
            cs((SG_HEADS * SG_BLOCK, SG_BLOCK)),
            cs((SG_BLOCK, W_B)),
            cs((CONV_K, w_c)), cs((1, w_c)),
            cs((1, w_c)), cs((1, w_c)),
            cs((d_model, d_model)),
            cs((1, d_model)),
        ],
        out_specs=x_spec,
        out_shape=jax.ShapeDtypeStruct(x.shape, x.dtype),
        scratch_shapes=[
            pltpu.VMEM((tm + POOL_HALO, W_A), F32),
            pltpu.VMEM((w_c // LANES, tm + CONV_HALO, LANES), F32),
            pltpu.VMEM((w_c // LANES, tm, LANES), F32),
            pltpu.VMEM((tm, d_model), BF16),
            pltpu.VMEM((tm, d_model), BF16),
            pltpu.VMEM((tm, d_in), F32),
        ],
        compiler_params=pltpu.CompilerParams(
            dimension_semantics=("arbitrary", "arbitrary"),
            vmem_limit_bytes=VMEM_LIMIT),
        name=f"mixer_l{layer}",
    )(x, p["pre_mix_g"], p["w_in"], p["b_in"], p["pool_w"], p["pool_scale"],
      p["sg_ln_g"], p["sg_ln_b"], p["sg_w"], p["sg_b"],
      p["conv_w"], p["conv_b"], p["conv_ln_g"], p["conv_ln_b"], p["w_out"], p["post_mix_g"])


def _attn_kernel(x_ref, g1_ref, wqk_ref, vwo_ref, g2_ref, o_ref, hn_ref, s_ref, p_ref, acc_ref):
    tm, d_model = x_ref.shape
    n_mem = d_model // X_HEADS
    hn_ref[...] = _rms(x_ref[...], g1_ref[...]).astype(BF16)
    s_ref[...] = _dot(hn_ref[...], wqk_ref[...])
    rows = 128
    for h in range(X_HEADS):
        hs = slice(h * n_mem, (h + 1) * n_mem)
        for r0 in range(0, tm, rows):
            s = s_ref[r0:r0 + rows, hs]
            e = jnp.exp(s - jnp.max(s, axis=-1, keepdims=True))
            p_ref[r0:r0 + rows, :] = (e / jnp.sum(e, axis=-1, keepdims=True)).astype(BF16)
        part = _dot(p_ref[...], vwo_ref[hs, :])
        if h == 0:
            acc_ref[...] = part
        else:
            acc_ref[...] += part
    o_ref[...] = x_ref[...] + _rms(acc_ref[...], g2_ref[...])


def _attention(x, layer, p, wqk, vwo):
    bsz, seq, d_model = x.shape
    tm = TM_ATT
    assert seq % tm == 0
    x_spec = pl.BlockSpec((None, tm, d_model), lambda b, i: (b, i, 0))
    kv_spec = pl.BlockSpec((None, None, d_model, d_model), lambda b, i: (layer, b, 0, 0))
    cs = functools.partial(_const_spec, layer=layer)
    return pl.pallas_call(
        _attn_kernel,
        grid=(bsz, seq // tm),
        in_specs=[x_spec, cs((1, d_model)), kv_spec, kv_spec, cs((1, d_model))],
        out_specs=x_spec,
        out_shape=jax.ShapeDtypeStruct(x.shape, x.dtype),
        scratch_shapes=[pltpu.VMEM((tm, d_model), BF16),
                        pltpu.VMEM((tm, d_model), F32),
                        pltpu.VMEM((tm, d_model // X_HEADS), BF16),
                        pltpu.VMEM((tm, d_model), F32)],
        compiler_params=pltpu.CompilerParams(
            dimension_semantics=("arbitrary", "arbitrary"),
            vmem_limit_bytes=VMEM_LIMIT),
        name=f"attn_l{layer}",
    )(x, p["pre_x_g"], wqk, vwo, p["post_x_g"])


def _ffn_kernel(x_ref, g1_ref, w1_ref, w2_ref, g2_ref, o_ref, hn_ref, hc_ref, acc_ref):
    d_ff = w1_ref.shape[1]
    hn_ref[...] = _rms(x_ref[...], g1_ref[...]).astype(BF16)
    for c in range(d_ff // FF_CHUNK):
        cs = slice(c * FF_CHUNK, (c + 1) * FF_CHUNK)
        a = jnp.maximum(_dot(hn_ref[...], w1_ref[:, cs].astype(BF16)), 0.0)
        hc_ref[...] = (a * a).astype(BF16)
        part = _dot(hc_ref[...], w2_ref[cs, :].astype(BF16))
        if c == 0:
            acc_ref[...] = part
        else:
            acc_ref[...] += part
    o_ref[...] = x_ref[...] + _rms(acc_ref[...], g2_ref[...])


def _ffn(x, layer, p):
    bsz, seq, d_model = x.shape
    d_ff = p["w_ff1"].shape[-1]
    tm = TM_FF
    assert seq % tm == 0 and d_ff % FF_CHUNK == 0
    x_spec = pl.BlockSpec((None, tm, d_model), lambda b, i: (b, i, 0))
    cs = functools.partial(_const_spec, layer=layer)
    return pl.pallas_call(
        _ffn_kernel,
        grid=(bsz, seq // tm),
        in_specs=[x_spec, cs((1, d_model)), cs((d_model, d_ff)), cs((d_ff, d_model)),
                  cs((1, d_model))],
        out_specs=x_spec,
        out_shape=jax.ShapeDtypeStruct(x.shape, x.dtype),
        scratch_shapes=[pltpu.VMEM((tm, d_model), BF16), pltpu.VMEM((tm, FF_CHUNK), BF16),
                        pltpu.VMEM((tm, d_model), F32)],
        compiler_params=pltpu.CompilerParams(
            dimension_semantics=("arbitrary", "arbitrary"),
            vmem_limit_bytes=VMEM_LIMIT),
        name=f"ffn_l{layer}",
    )(x, p["pre_ff_g"], p["w_ff1"], p["w_ff2"], p["post_ff_g"])


def kernel(x, mem, pre_mix_g, w_in, b_in, pool_w, pool_scale, sg_ln_g, sg_ln_b, sg_w, sg_b, conv_w, conv_b, conv_ln_g, conv_ln_b, w_out, post_mix_g, pre_x_g, mem_g, wq, wk, wv, wo, post_x_g, pre_ff_g, w_ff1, w_ff2, post_ff_g):
    n_layers = w_in.shape[0]
    n_groups = pool_w.shape[1]

    def row(v):
        return v[:, None, :]

    eye = jnp.eye(n_groups, dtype=pool_w.dtype)
    pool_bd = jnp.einsum("lgcd,gh->lgchd", pool_w, eye).reshape(n_layers, W_A, W_A)
    sg_bias = jnp.repeat(jnp.swapaxes(sg_b, 1, 2), SG_HEAD_DIM, axis=2)
    p = {
        "pre_mix_g": row(pre_mix_g), "w_in": w_in, "b_in": row(b_in),
        "pool_w": pool_bd.astype(BF16), "pool_scale": row(pool_scale),
        "sg_ln_g": row(sg_ln_g), "sg_ln_b": row(sg_ln_b),
        "sg_w": sg_w.reshape(n_layers, SG_HEADS * SG_BLOCK, SG_BLOCK), "sg_b": sg_bias,
        "conv_w": conv_w, "conv_b": row(conv_b),
        "conv_ln_g": row(conv_ln_g), "conv_ln_b": row(conv_ln_b),
        "w_out": w_out, "post_mix_g": row(post_mix_g),
        "pre_x_g": row(pre_x_g), "post_x_g": row(post_x_g),
        "pre_ff_g": row(pre_ff_g), "w_ff1": w_ff1, "w_ff2": w_ff2,
        "post_ff_g": row(post_ff_g),
    }
    wqk, vwo = _fold_memory(mem, row(mem_g), wq, wk, wv, wo)
    for layer in range(n_layers):
        x = _mixer(x, layer, p)
        x = _attention(x, layer, p, wqk, vwo)
        x = _ffn(x, layer, p)
    return x
```

```python
import functools
import math

import jax
import jax.numpy as jnp
from jax import lax
from jax.experimental import pallas as pl
from jax.experimental.pallas import tpu as pltpu

EPS = 1e-6
BF16 = jnp.bfloat16
F32 = jnp.float32

POOL_WINDOWS = (2, 4, 8, 16)
POOL_CH = 64
W_A = len(POOL_WINDOWS) * POOL_CH
SG_BLOCK = 128
SG_HEADS = 4
SG_HEAD_DIM = 96
W_B = SG_HEADS * SG_HEAD_DIM
CONV_K = 31
X_HEADS = 4

TM_MIX = 512
TM_ATT = 1024
TM_FF = 512
SG_GROUP = 4
LANES = 128
CONV_ROW_STRIDE = 2
FF_CHUNK = 1024
POOL_HALO = 16
CONV_HALO = 32
VMEM_LIMIT = 56 * 1024 * 1024


def _dot(a, b):
    return jnp.dot(a, b, preferred_element_type=F32)


def _rms(xf, g):
    ms = jnp.mean(xf * xf, axis=-1, keepdims=True)
    return xf * lax.rsqrt(ms + EPS) * g


def _layer_norm(xf, g, b):
    mu = jnp.mean(xf, axis=-1, keepdims=True)
    xc = xf - mu
    var = jnp.mean(xc * xc, axis=-1, keepdims=True)
    return xc * lax.rsqrt(var + EPS) * g + b


def _kv_kernel(mem_ref, g_ref, wq_ref, wk_ref, wv_ref, wo_ref, wqk_ref, vwo_ref):
    d_model = mem_ref.shape[-1]
    dh = d_model // X_HEADS
    scale = 1.0 / math.sqrt(dh)
    m = _rms(mem_ref[...], g_ref[...]).astype(BF16)
    k = _dot(m, wk_ref[...].astype(BF16)).astype(BF16)
    v = _dot(m, wv_ref[...].astype(BF16)).astype(BF16)
    for h in range(X_HEADS):
        hs = slice(h * dh, (h + 1) * dh)
        s = lax.dot_general(wq_ref[:, hs].astype(BF16), k[:, hs], (((1,), (1,)), ((), ())),
                            preferred_element_type=F32)
        wqk_ref[:, hs] = (s * scale).astype(BF16)
        vwo_ref[hs, :] = _dot(v[:, hs], wo_ref[hs, :].astype(BF16)).astype(BF16)


def _fold_memory(mem, mem_g, wq, wk, wv, wo):
    n_layers, d_model, _ = wq.shape
    bsz, n_mem, _ = mem.shape
    assert n_mem * X_HEADS == d_model
    w_spec = pl.BlockSpec((None, d_model, d_model), lambda l, b: (l, 0, 0))
    out_spec = pl.BlockSpec((None, None, d_model, d_model), lambda l, b: (l, b, 0, 0))
    out_shape = jax.ShapeDtypeStruct((n_layers, bsz, d_model, d_model), BF16)
    return pl.pallas_call(
        _kv_kernel,
        grid=(n_layers, bsz),
        in_specs=[
            pl.BlockSpec((None, n_mem, d_model), lambda l, b: (b, 0, 0)),
            pl.BlockSpec((None, 1, d_model), lambda l, b: (l, 0, 0)),
            w_spec, w_spec, w_spec, w_spec,
        ],
        out_specs=[out_spec, out_spec],
        out_shape=[out_shape, out_shape],
        compiler_params=pltpu.CompilerParams(
            dimension_semantics=("arbitrary", "arbitrary"),
            vmem_limit_bytes=VMEM_LIMIT),
        name="fold_memory",
    )(mem, mem_g, wq, wk, wv, wo)


def _mixer_kernel(x_ref, g1_ref, win_ref, bin_ref, poolw_ref, pscale_ref,
                  sglng_ref, sglnb_ref, sgw_ref, sgbias_ref,
                  convw_ref, convb_ref, clng_ref, clnb_ref, wout_ref, g2_ref,
                  o_ref, abuf, hbuf, cbuf, ycat, hn_ref, z_ref, vcat, sgres):
    tm = x_ref.shape[0]
    n_slab = hbuf.shape[0]
    w_c = n_slab * LANES
    off_b = W_A
    off_c = W_A + 2 * W_B
    i = pl.program_id(1)

    @pl.when(i == 0)
    def _():
        abuf[0:POOL_HALO, :] = jnp.zeros((POOL_HALO, W_A), F32)
        hbuf[:, 0:CONV_HALO, :] = jnp.zeros((n_slab, CONV_HALO, LANES), F32)

    hn_ref[...] = _rms(x_ref[...], g1_ref[...]).astype(BF16)
    z_ref[...] = _dot(hn_ref[...], win_ref[...].astype(BF16)) + bin_ref[...]

    abuf[POOL_HALO:POOL_HALO + tm, :] = z_ref[:, 0:W_A]
    rows = 128
    lane = lax.broadcasted_iota(jnp.int32, (rows, W_A), 1)
    row = lax.broadcasted_iota(jnp.int32, (rows, W_A), 0)
    group = lane // POOL_CH
    window = jnp.full((rows, W_A), POOL_WINDOWS[-1], jnp.int32)
    for gi in range(len(POOL_WINDOWS) - 2, -1, -1):
        window = jnp.where(group == gi, POOL_WINDOWS[gi], window)
    for r0 in range(0, tm, rows):
        s = abuf[r0:r0 + POOL_HALO + rows, :]
        a0 = s[POOL_HALO:]
        win_sum = None
        w = 1
        for gi, target in enumerate(POOL_WINDOWS):
            while w < target:
                s = s + pltpu.roll(s, w, 0)
                w *= 2
            cur = s[POOL_HALO:]
            win_sum = cur if win_sum is None else jnp.where(group >= gi, cur, win_sum)
        pos1 = i * tm + r0 + row + 1
        cnt = jnp.minimum(pos1, window).astype(F32)
        p = (win_sum / cnt - a0).astype(BF16)
        ya = _dot(p, poolw_ref[...]) * pscale_ref[...]
        ycat[r0:r0 + rows, 0:W_A] = ya.astype(BF16)
    abuf[0:POOL_HALO, :] = abuf[tm:tm + POOL_HALO, :]

    wrow = lax.broadcasted_iota(jnp.int32, (SG_HEADS * SG_BLOCK, SG_BLOCK), 0)
    wcol = lax.broadcasted_iota(jnp.int32, (SG_HEADS * SG_BLOCK, SG_BLOCK), 1)
    causal = (wrow & (SG_BLOCK - 1)) >= wcol
    wst = jnp.where(causal, sgw_ref[...], 0.0).astype(BF16)
    hl = lax.broadcasted_iota(jnp.int32, (SG_BLOCK, W_B), 1)
    for g0 in range(0, tm // SG_BLOCK, SG_GROUP):
        for j in range(SG_GROUP):
            rs = slice((g0 + j) * SG_BLOCK, (g0 + j + 1) * SG_BLOCK)
            v = _layer_norm(jax.nn.gelu(z_ref[rs, off_b + W_B:off_b + 2 * W_B]),
                            sglng_ref[...], sglnb_ref[...])
            vcat[:, j * W_B:(j + 1) * W_B] = v.astype(BF16)
        sgres[...] = _dot(wst, vcat[...])
        for j in range(SG_GROUP):
            rs = slice((g0 + j) * SG_BLOCK, (g0 + j + 1) * SG_BLOCK)
            cols = slice(j * W_B, (j + 1) * W_B)
            sv = sgres[(SG_HEADS - 1) * SG_BLOCK:SG_HEADS * SG_BLOCK, cols]
            for h in range(SG_HEADS - 2, -1, -1):
                sv = jnp.where(hl < (h + 1) * SG_HEAD_DIM, sgres[h * SG_BLOCK:(h + 1) * SG_BLOCK, cols], sv)
            yb = jax.nn.gelu(z_ref[rs, off_b:off_b + W_B]) * (sv + sgbias_ref[...])
            ycat[rs, W_A:W_A + W_B] = yb.astype(BF16)

    grows = 64
    for r0 in range(0, tm, grows):
        rs = slice(r0, r0 + grows)
        glu = z_ref[rs, off_c:off_c + w_c] * jax.nn.sigmoid(z_ref[rs, off_c + w_c:off_c + 2 * w_c])
        for c in range(n_slab):
            hbuf[c, CONV_HALO + r0:CONV_HALO + r0 + grows, :] = glu[:, c * LANES:(c + 1) * LANES]
    lead = CONV_HALO - (CONV_K - 1)
    crows = 256
    nrow = crows // CONV_ROW_STRIDE
    for c in range(n_slab):
        cs = slice(c * LANES, (c + 1) * LANES)
        for r0 in range(0, tm, crows):
            for ph in range(CONV_ROW_STRIDE):
                acc = None
                for k in range(CONV_K):
                    t = (hbuf[c, pl.ds(r0 + ph + lead + k, nrow, stride=CONV_ROW_STRIDE), :]
                         * convw_ref[k:k + 1, cs])
                    acc = t if acc is None else acc + t
                cbuf[c, pl.ds(r0 + ph, nrow, stride=CONV_ROW_STRIDE), :] = acc + convb_ref[:, cs]
    lrows = 128
    for r0 in range(0, tm, lrows):
        conv = jnp.concatenate([cbuf[c, r0:r0 + lrows, :] for c in range(n_slab)], axis=-1)
        hc = _layer_norm(conv, clng_ref[...], clnb_ref[...])
        ycat[r0:r0 + lrows, W_A + W_B:W_A + W_B + w_c] = jax.nn.silu(hc).astype(BF16)
    hbuf[:, 0:CONV_HALO, :] = hbuf[:, tm:tm + CONV_HALO, :]

    y = _dot(ycat[...], wout_ref[...].astype(BF16))
    o_ref[...] = x_ref[...] + _rms(y, g2_ref[...])


def _const_spec(shape, layer):
    nd = len(shape)
    return pl.BlockSpec((None,) + tuple(shape), lambda b, i: (layer,) + (0,) * nd,
                        pipeline_mode=pl.Buffered(1))


def _mixer(x, layer, p):
    bsz, seq, d_model = x.shape
    tm = TM_MIX
    assert seq % tm == 0 and tm % (SG_BLOCK * SG_GROUP) == 0
    d_in = p["w_in"].shape[-1]
    w_c = (d_in - W_A - 2 * W_B) // 2
    assert W_A + W_B + w_c == d_model and w_c % LANES == 0
    assert CONV_HALO >= CONV_K - 1 and tm % 256 == 0
    x_spec = pl.BlockSpec((None, tm, d_model), lambda b, i: (b, i, 0))
    cs = functools.partial(_const_spec, layer=layer)
    return pl.pallas_call(
        _mixer_kernel,
        grid=(bsz, seq // tm),
        in_specs=[
            x_spec,
            cs((1, d_model)),
            cs((d_model, d_in)),
            cs((1, d_in)),
            cs((W_A, W_A)),
            cs((1, W_A)),
            cs((1, W_B)), cs((1, W_B)),
            cs((SG_HEADS * SG_BLOCK, SG_BLOCK)),
            cs((SG_BLOCK, W_B)),
            cs((CONV_K, w_c)), cs((1, w_c)),
            cs((1, w_c)), cs((1, w_c)),
            cs((d_model, d_model)),
            cs((1, d_model)),
        ],
        out_specs=x_spec,
        out_shape=jax.ShapeDtypeStruct(x.shape, x.dtype),
        scratch_shapes=[
            pltpu.VMEM((tm + POOL_HALO, W_A), F32),
            pltpu.VMEM((w_c // LANES, tm + CONV_HALO, LANES), F32),
            pltpu.VMEM((w_c // LANES, tm, LANES), F32),
            pltpu.VMEM((tm, d_model), BF16),
            pltpu.VMEM((tm, d_model), BF16),
            pltpu.VMEM((tm, d_in), F32),
            pltpu.VMEM((SG_BLOCK, SG_GROUP * W_B), BF16),
            pltpu.VMEM((SG_HEADS * SG_BLOCK, SG_GROUP * W_B), F32),
        ],
        compiler_params=pltpu.CompilerParams(
            dimension_semantics=("arbitrary", "arbitrary"),
            vmem_limit_bytes=VMEM_LIMIT),
        name=f"mixer_l{layer}",
    )(x, p["pre_mix_g"], p["w_in"], p["b_in"], p["pool_w"], p["pool_scale"],
      p["sg_ln_g"], p["sg_ln_b"], p["sg_w"], p["sg_b"],
      p["conv_w"], p["conv_b"], p["conv_ln_g"], p["conv_ln_b"], p["w_out"], p["post_mix_g"])


def _attn_kernel(x_ref, g1_ref, wqk_ref, vwo_ref, g2_ref, o_ref, hn_ref, s_ref, p_ref, acc_ref):
    tm, d_model = x_ref.shape
    n_mem = d_model // X_HEADS
    hn_ref[...] = _rms(x_ref[...], g1_ref[...]).astype(BF16)
    s_ref[...] = _dot(hn_ref[...], wqk_ref[...])
    rows = 128
    for h in range(X_HEADS):
        hs = slice(h * n_mem, (h + 1) * n_mem)
        for r0 in range(0, tm, rows):
            s = s_ref[r0:r0 + rows, hs]
            e = jnp.exp(s - jnp.max(s, axis=-1, keepdims=True))
            p_ref[r0:r0 + rows, :] = (e / jnp.sum(e, axis=-1, keepdims=True)).astype(BF16)
        part = _dot(p_ref[...], vwo_ref[hs, :])
        if h == 0:
            acc_ref[...] = part
        else:
            acc_ref[...] += part
    o_ref[...] = x_ref[...] + _rms(acc_ref[...], g2_ref[...])


def _attention(x, layer, p, wqk, vwo):
    bsz, seq, d_model = x.shape
    tm = TM_ATT
    assert seq % tm == 0
    x_spec = pl.BlockSpec((None, tm, d_model), lambda b, i: (b, i, 0))
    kv_spec = pl.BlockSpec((None, None, d_model, d_model), lambda b, i: (layer, b, 0, 0))
    cs = functools.partial(_const_spec, layer=layer)
    return pl.pallas_call(
        _attn_kernel,
        grid=(bsz, seq // tm),
        in_specs=[x_spec, cs((1, d_model)), kv_spec, kv_spec, cs((1, d_model))],
        out_specs=x_spec,
        out_shape=jax.ShapeDtypeStruct(x.shape, x.dtype),
        scratch_shapes=[pltpu.VMEM((tm, d_model), BF16),
                        pltpu.VMEM((tm, d_model), F32),
                        pltpu.VMEM((tm, d_model // X_HEADS), BF16),
                        pltpu.VMEM((tm, d_model), F32)],
        compiler_params=pltpu.CompilerParams(
            dimension_semantics=("arbitrary", "arbitrary"),
            vmem_limit_bytes=VMEM_LIMIT),
        name=f"attn_l{layer}",
    )(x, p["pre_x_g"], wqk, vwo, p["post_x_g"])


def _ffn_kernel(x_ref, g1_ref, w1_ref, w2_ref, g2_ref, o_ref, hn_ref, hc_ref, acc_ref):
    d_ff = w1_ref.shape[1]
    hn_ref[...] = _rms(x_ref[...], g1_ref[...]).astype(BF16)
    for c in range(d_ff // FF_CHUNK):
        cs = slice(c * FF_CHUNK, (c + 1) * FF_CHUNK)
        a = jnp.maximum(_dot(hn_ref[...], w1_ref[:, cs].astype(BF16)), 0.0)
        hc_ref[...] = (a * a).astype(BF16)
        part = _dot(hc_ref[...], w2_ref[cs, :].astype(BF16))
        if c == 0:
            acc_ref[...] = part
        else:
            acc_ref[...] += part
    o_ref[...] = x_ref[...] + _rms(acc_ref[...], g2_ref[...])


def _ffn(x, layer, p):
    bsz, seq, d_model = x.shape
    d_ff = p["w_ff1"].shape[-1]
    tm = TM_FF
    assert seq % tm == 0 and d_ff % FF_CHUNK == 0
    x_spec = pl.BlockSpec((None, tm, d_model), lambda b, i: (b, i, 0))
    cs = functools.partial(_const_spec, layer=layer)
    return pl.pallas_call(
        _ffn_kernel,
        grid=(bsz, seq // tm),
        in_specs=[x_spec, cs((1, d_model)), cs((d_model, d_ff)), cs((d_ff, d_model)),
                  cs((1, d_model))],
        out_specs=x_spec,
        out_shape=jax.ShapeDtypeStruct(x.shape, x.dtype),
        scratch_shapes=[pltpu.VMEM((tm, d_model), BF16), pltpu.VMEM((tm, FF_CHUNK), BF16),
                        pltpu.VMEM((tm, d_model), F32)],
        compiler_params=pltpu.CompilerParams(
            dimension_semantics=("arbitrary", "arbitrary"),
            vmem_limit_bytes=VMEM_LIMIT),
        name=f"ffn_l{layer}",
    )(x, p["pre_ff_g"], p["w_ff1"], p["w_ff2"], p["post_ff_g"])


def kernel(x, mem, pre_mix_g, w_in, b_in, pool_w, pool_scale, sg_ln_g, sg_ln_b, sg_w, sg_b, conv_w, conv_b, conv_ln_g, conv_ln_b, w_out, post_mix_g, pre_x_g, mem_g, wq, wk, wv, wo, post_x_g, pre_ff_g, w_ff1, w_ff2, post_ff_g):
    n_layers = w_in.shape[0]
    n_groups = pool_w.shape[1]

    def row(v):
        return v[:, None, :]

    eye = jnp.eye(n_groups, dtype=pool_w.dtype)
    pool_bd = jnp.einsum("lgcd,gh->lgchd", pool_w, eye).reshape(n_layers, W_A, W_A)
    sg_bias = jnp.repeat(jnp.swapaxes(sg_b, 1, 2), SG_HEAD_DIM, axis=2)
    p = {
        "pre_mix_g": row(pre_mix_g), "w_in": w_in, "b_in": row(b_in),
        "pool_w": pool_bd.astype(BF16), "pool_scale": row(pool_scale),
        "sg_ln_g": row(sg_ln_g), "sg_ln_b": row(sg_ln_b),
        "sg_w": sg_w.reshape(n_layers, SG_HEADS * SG_BLOCK, SG_BLOCK), "sg_b": sg_bias,
        "conv_w": conv_w, "conv_b": row(conv_b),
        "conv_ln_g": row(conv_ln_g), "conv_ln_b": row(conv_ln_b),
        "w_out": w_out, "post_mix_g": row(post_mix_g),
        "pre_x_g": row(pre_x_g), "post_x_g": row(post_x_g),
        "pre_ff_g": row(pre_ff_g), "w_ff1": w_ff1, "w_ff2": w_ff2,
        "post_ff_g": row(post_ff_g),
    }
    wqk, vwo = _fold_memory(mem, row(mem_g), wq, wk, wv, wo)
    for layer in range(n_layers):
        x = _mixer(x, layer, p)
        x = _attention(x, layer, p, wqk, vwo)
        x = _ffn(x, layer, p)
    return x
```

```python
import functools
import math

import jax
import jax.numpy as jnp
from jax import lax
from jax.experimental import pallas as pl
from jax.experimental.pallas import tpu as pltpu

EPS = 1e-6
BF16 = jnp.bfloat16
F32 = jnp.float32

POOL_WINDOWS = (2, 4, 8, 16)
POOL_CH = 64
W_A = len(POOL_WINDOWS) * POOL_CH
SG_BLOCK = 128
SG_HEADS = 4
SG_HEAD_DIM = 96
W_B = SG_HEADS * SG_HEAD_DIM
CONV_K = 31
X_HEADS = 4

TM_MIX = 1024
TM_ATT = 512
TM_FF = 512
LANES = 128
CONV_ROW_STRIDE = 2
FF_CHUNK = 1024
POOL_HALO = 16
CONV_HALO = 32
VMEM_LIMIT = 56 * 1024 * 1024


def _dot(a, b):
    return jnp.dot(a, b, preferred_element_type=F32)


def _rms(xf, g):
    ms = jnp.mean(xf * xf, axis=-1, keepdims=True)
    return xf * lax.rsqrt(ms + EPS) * g


def _layer_norm(xf, g, b):
    mu = jnp.mean(xf, axis=-1, keepdims=True)
    xc = xf - mu
    var = jnp.mean(xc * xc, axis=-1, keepdims=True)
    return xc * lax.rsqrt(var + EPS) * g + b


def _kv_kernel(mem_ref, g_ref, wq_ref, wk_ref, wv_ref, wo_ref, wqk_ref, vwo_ref):
    d_model = mem_ref.shape[-1]
    dh = d_model // X_HEADS
    scale = 1.0 / math.sqrt(dh)
    m = _rms(mem_ref[...], g_ref[...]).astype(BF16)
    k = _dot(m, wk_ref[...].astype(BF16)).astype(BF16)
    v = _dot(m, wv_ref[...].astype(BF16)).astype(BF16)
    for h in range(X_HEADS):
        hs = slice(h * dh, (h + 1) * dh)
        s = lax.dot_general(wq_ref[:, hs].astype(BF16), k[:, hs], (((1,), (1,)), ((), ())),
                            preferred_element_type=F32)
        wqk_ref[:, hs] = (s * scale).astype(BF16)
        vwo_ref[hs, :] = _dot(v[:, hs], wo_ref[hs, :].astype(BF16)).astype(BF16)


def _fold_memory(mem, mem_g, wq, wk, wv, wo):
    n_layers, d_model, _ = wq.shape
    bsz, n_mem, _ = mem.shape
    assert n_mem * X_HEADS == d_model
    w_spec = pl.BlockSpec((None, d_model, d_model), lambda l, b: (l, 0, 0))
    out_spec = pl.BlockSpec((None, None, d_model, d_model), lambda l, b: (l, b, 0, 0))
    out_shape = jax.ShapeDtypeStruct((n_layers, bsz, d_model, d_model), BF16)
    return pl.pallas_call(
        _kv_kernel,
        grid=(n_layers, bsz),
        in_specs=[
            pl.BlockSpec((None, n_mem, d_model), lambda l, b: (b, 0, 0)),
            pl.BlockSpec((None, 1, d_model), lambda l, b: (l, 0, 0)),
            w_spec, w_spec, w_spec, w_spec,
        ],
        out_specs=[out_spec, out_spec],
        out_shape=[out_shape, out_shape],
        compiler_params=pltpu.CompilerParams(
            dimension_semantics=("arbitrary", "arbitrary"),
            vmem_limit_bytes=VMEM_LIMIT),
        name="fold_memory",
    )(mem, mem_g, wq, wk, wv, wo)


def _mixer_kernel(x_ref, g1_ref, win_ref, bin_ref, poolw_ref, pscale_ref,
                  sglng_ref, sglnb_ref, sgw_ref, sgbias_ref,
                  convw_ref, convb_ref, clng_ref, clnb_ref, wout_ref, g2_ref,
                  o_ref, abuf, ycat, hn_ref, z_ref, *slabs):
    tm = x_ref.shape[0]
    n_slab = len(slabs) // 2
    hbufs, cbufs = slabs[:n_slab], slabs[n_slab:]
    w_c = n_slab * LANES
    off_b = W_A
    off_c = W_A + 2 * W_B
    i = pl.program_id(1)

    @pl.when(i == 0)
    def _():
        abuf[0:POOL_HALO, :] = jnp.zeros((POOL_HALO, W_A), F32)
        for hb in hbufs:
            hb[0:CONV_HALO, :] = jnp.zeros((CONV_HALO, LANES), F32)

    hn_ref[...] = _rms(x_ref[...], g1_ref[...]).astype(BF16)
    z_ref[...] = _dot(hn_ref[...], win_ref[...].astype(BF16)) + bin_ref[...]

    abuf[POOL_HALO:POOL_HALO + tm, :] = z_ref[:, 0:W_A]
    rows = 128
    lane = lax.broadcasted_iota(jnp.int32, (rows, W_A), 1)
    row = lax.broadcasted_iota(jnp.int32, (rows, W_A), 0)
    group = lane // POOL_CH
    window = jnp.full((rows, W_A), POOL_WINDOWS[-1], jnp.int32)
    for gi in range(len(POOL_WINDOWS) - 2, -1, -1):
        window = jnp.where(group == gi, POOL_WINDOWS[gi], window)
    for r0 in range(0, tm, rows):
        s = abuf[r0:r0 + POOL_HALO + rows, :]
        a0 = s[POOL_HALO:]
        win_sum = None
        w = 1
        for gi, target in enumerate(POOL_WINDOWS):
            while w < target:
                s = s + pltpu.roll(s, w, 0)
                w *= 2
            cur = s[POOL_HALO:]
            win_sum = cur if win_sum is None else jnp.where(group >= gi, cur, win_sum)
        pos1 = i * tm + r0 + row + 1
        cnt = jnp.minimum(pos1, window).astype(F32)
        p = (win_sum / cnt - a0).astype(BF16)
        ya = _dot(p, poolw_ref[...]) * pscale_ref[...]
        ycat[r0:r0 + rows, 0:W_A] = ya.astype(BF16)
    abuf[0:POOL_HALO, :] = abuf[tm:tm + POOL_HALO, :]

    wrow = lax.broadcasted_iota(jnp.int32, (SG_HEADS * SG_BLOCK, SG_BLOCK), 0)
    wcol = lax.broadcasted_iota(jnp.int32, (SG_HEADS * SG_BLOCK, SG_BLOCK), 1)
    causal = (wrow & (SG_BLOCK - 1)) >= wcol
    wst = jnp.where(causal, sgw_ref[...], 0.0).astype(BF16)
    hl = lax.broadcasted_iota(jnp.int32, (SG_BLOCK, W_B), 1)
    for blk in range(tm // SG_BLOCK):
        rs = slice(blk * SG_BLOCK, (blk + 1) * SG_BLOCK)
        v = _layer_norm(jax.nn.gelu(z_ref[rs, off_b + W_B:off_b + 2 * W_B]),
                        sglng_ref[...], sglnb_ref[...]).astype(BF16)
        res = _dot(wst, v)
        sv = res[(SG_HEADS - 1) * SG_BLOCK:SG_HEADS * SG_BLOCK]
        for h in range(SG_HEADS - 2, -1, -1):
            sv = jnp.where(hl < (h + 1) * SG_HEAD_DIM, res[h * SG_BLOCK:(h + 1) * SG_BLOCK], sv)
        yb = jax.nn.gelu(z_ref[rs, off_b:off_b + W_B]) * (sv + sgbias_ref[...])
        ycat[rs, W_A:W_A + W_B] = yb.astype(BF16)

    grows = 64
    for r0 in range(0, tm, grows):
        rs = slice(r0, r0 + grows)
        glu = z_ref[rs, off_c:off_c + w_c] * jax.nn.sigmoid(z_ref[rs, off_c + w_c:off_c + 2 * w_c])
        for c in range(n_slab):
            hbufs[c][CONV_HALO + r0:CONV_HALO + r0 + grows, :] = glu[:, c * LANES:(c + 1) * LANES]
    lead = CONV_HALO - (CONV_K - 1)
    crows = 256
    nrow = crows // CONV_ROW_STRIDE
    for c in range(n_slab):
        cs = slice(c * LANES, (c + 1) * LANES)
        for r0 in range(0, tm, crows):
            for ph in range(CONV_ROW_STRIDE):
                acc = None
                for k in range(CONV_K):
                    t = (hbufs[c][pl.ds(r0 + ph + lead + k, nrow, stride=CONV_ROW_STRIDE), :]
                         * convw_ref[k:k + 1, cs])
                    acc = t if acc is None else acc + t
                cbufs[c][pl.ds(r0 + ph, nrow, stride=CONV_ROW_STRIDE), :] = acc + convb_ref[:, cs]
    lrows = 128
    for r0 in range(0, tm, lrows):
        conv = jnp.concatenate([cbufs[c][r0:r0 + lrows, :] for c in range(n_slab)], axis=-1)
        hc = _layer_norm(conv, clng_ref[...], clnb_ref[...])
        ycat[r0:r0 + lrows, W_A + W_B:W_A + W_B + w_c] = jax.nn.silu(hc).astype(BF16)
    for c in range(n_slab):
        hbufs[c][0:CONV_HALO, :] = hbufs[c][tm:tm + CONV_HALO, :]

    y = _dot(ycat[...], wout_ref[...].astype(BF16))
    o_ref[...] = x_ref[...] + _rms(y, g2_ref[...])


def _const_spec(shape, layer):
    nd = len(shape)
    return pl.BlockSpec((None,) + tuple(shape), lambda b, i: (layer,) + (0,) * nd,
                        pipeline_mode=pl.Buffered(1))


def _mixer(x, layer, p):
    bsz, seq, d_model = x.shape
    tm = TM_MIX
    assert seq % tm == 0 and tm % SG_BLOCK == 0
    d_in = p["w_in"].shape[-1]
    w_c = (d_in - W_A - 2 * W_B) // 2
    assert W_A + W_B + w_c == d_model and w_c % LANES == 0
    n_slab = w_c // LANES
    assert CONV_HALO >= CONV_K - 1 and tm % 256 == 0
    x_spec = pl.BlockSpec((None, tm, d_model), lambda b, i: (b, i, 0))
    cs = functools.partial(_const_spec, layer=layer)
    return pl.pallas_call(
        _mixer_kernel,
        grid=(bsz, seq // tm),
        in_specs=[
            x_spec,
            cs((1, d_model)),
            cs((d_model, d_in)),
            cs((1, d_in)),
            cs((W_A, W_A)),
            cs((1, W_A)),
            cs((1, W_B)), cs((1, W_B)),
            cs((SG_HEADS * SG_BLOCK, SG_BLOCK)),
            cs((SG_BLOCK, W_B)),
            cs((CONV_K, w_c)), cs((1, w_c)),
            cs((1, w_c)), cs((1, w_c)),
            cs((d_model, d_model)),
            cs((1, d_model)),
        ],
        out_specs=x_spec,
        out_shape=jax.ShapeDtypeStruct(x.shape, x.dtype),
        scratch_shapes=[
            pltpu.VMEM((tm + POOL_HALO, W_A), F32),
            pltpu.VMEM((tm, d_model), BF16),
            pltpu.VMEM((tm, d_model), BF16),
            pltpu.VMEM((tm, d_in), F32),
        ] + [pltpu.VMEM((tm + CONV_HALO, LANES), F32)] * n_slab
          + [pltpu.VMEM((tm, LANES), F32)] * n_slab,
        compiler_params=pltpu.CompilerParams(
            dimension_semantics=("arbitrary", "arbitrary"),
            vmem_limit_bytes=VMEM_LIMIT),
        name=f"mixer_l{layer}",
    )(x, p["pre_mix_g"], p["w_in"], p["b_in"], p["pool_w"], p["pool_scale"],
      p["sg_ln_g"], p["sg_ln_b"], p["sg_w"], p["sg_b"],
      p["conv_w"], p["conv_b"], p["conv_ln_g"], p["conv_ln_b"], p["w_out"], p["post_mix_g"])


def _attn_kernel(x_ref, g1_ref, wqk_ref, vwo_ref, g2_ref, o_ref, hn_ref, s_ref, p_ref, acc_ref):
    tm, d_model = x_ref.shape
    n_mem = d_model // X_HEADS
    hn_ref[...] = _rms(x_ref[...], g1_ref[...]).astype(BF16)
    s_ref[...] = _dot(hn_ref[...], wqk_ref[...])
    rows = 128
    for h in range(X_HEADS):
        hs = slice(h * n_mem, (h + 1) * n_mem)
        for r0 in range(0, tm, rows):
            s = s_ref[r0:r0 + rows, hs]
            e = jnp.exp(s - jnp.max(s, axis=-1, keepdims=True))
            p_ref[r0:r0 + rows, :] = (e / jnp.sum(e, axis=-1, keepdims=True)).astype(BF16)
        part = _dot(p_ref[...], vwo_ref[hs, :])
        if h == 0:
            acc_ref[...] = part
        else:
            acc_ref[...] += part
    o_ref[...] = x_ref[...] + _rms(acc_ref[...], g2_ref[...])


def _attention(x, layer, p, wqk, vwo):
    bsz, seq, d_model = x.shape
    tm = TM_ATT
    assert seq % tm == 0
    x_spec = pl.BlockSpec((None, tm, d_model), lambda b, i: (b, i, 0))
    kv_spec = pl.BlockSpec((None, None, d_model, d_model), lambda b, i: (layer, b, 0, 0))
    cs = functools.partial(_const_spec, layer=layer)
    return pl.pallas_call(
        _attn_kernel,
        grid=(bsz, seq // tm),
        in_specs=[x_spec, cs((1, d_model)), kv_spec, kv_spec, cs((1, d_model))],
        out_specs=x_spec,
        out_shape=jax.ShapeDtypeStruct(x.shape, x.dtype),
        scratch_shapes=[pltpu.VMEM((tm, d_model), BF16),
                        pltpu.VMEM((tm, d_model), F32),
                        pltpu.VMEM((tm, d_model // X_HEADS), BF16),
                        pltpu.VMEM((tm, d_model), F32)],
        compiler_params=pltpu.CompilerParams(
            dimension_semantics=("arbitrary", "arbitrary"),
            vmem_limit_bytes=VMEM_LIMIT),
        name=f"attn_l{layer}",
    )(x, p["pre_x_g"], wqk, vwo, p["post_x_g"])


def _ffn_kernel(x_ref, g1_ref, w1_ref, w2_ref, g2_ref, o_ref, hn_ref, hc_ref, acc_ref):
    d_ff = w1_ref.shape[1]
    hn_ref[...] = _rms(x_ref[...], g1_ref[...]).astype(BF16)
    for c in range(d_ff // FF_CHUNK):
        cs = slice(c * FF_CHUNK, (c + 1) * FF_CHUNK)
        a = jnp.maximum(_dot(hn_ref[...], w1_ref[:, cs].astype(BF16)), 0.0)
        hc_ref[...] = (a * a).astype(BF16)
        part = _dot(hc_ref[...], w2_ref[cs, :].astype(BF16))
        if c == 0:
            acc_ref[...] = part
        else:
            acc_ref[...] += part
    o_ref[...] = x_ref[...] + _rms(acc_ref[...], g2_ref[...])


def _ffn(x, layer, p):
    bsz, seq, d_model = x.shape
    d_ff = p["w_ff1"].shape[-1]
    tm = TM_FF
    assert seq % tm == 0 and d_ff % FF_CHUNK == 0
    x_spec = pl.BlockSpec((None, tm, d_model), lambda b, i: (b, i, 0))
    cs = functools.partial(_const_spec, layer=layer)
    return pl.pallas_call(
        _ffn_kernel,
        grid=(bsz, seq // tm),
        in_specs=[x_spec, cs((1, d_model)), cs((d_model, d_ff)), cs((d_ff, d_model)),
                  cs((1, d_model))],
        out_specs=x_spec,
        out_shape=jax.ShapeDtypeStruct(x.shape, x.dtype),
        scratch_shapes=[pltpu.VMEM((tm, d_model), BF16), pltpu.VMEM((tm, FF_CHUNK), BF16),
                        pltpu.VMEM((tm, d_model), F32)],
        compiler_params=pltpu.CompilerParams(
            dimension_semantics=("arbitrary", "arbitrary"),
            vmem_limit_bytes=VMEM_LIMIT),
        name=f"ffn_l{layer}",
    )(x, p["pre_ff_g"], p["w_ff1"], p["w_ff2"], p["post_ff_g"])


def kernel(x, mem, pre_mix_g, w_in, b_in, pool_w, pool_scale, sg_ln_g, sg_ln_b, sg_w, sg_b, conv_w, conv_b, conv_ln_g, conv_ln_b, w_out, post_mix_g, pre_x_g, mem_g, wq, wk, wv, wo, post_x_g, pre_ff_g, w_ff1, w_ff2, post_ff_g):
    n_layers = w_in.shape[0]
    n_groups = pool_w.shape[1]

    def row(v):
        return v[:, None, :]

    eye = jnp.eye(n_groups, dtype=pool_w.dtype)
    pool_bd = jnp.einsum("lgcd,gh->lgchd", pool_w, eye).reshape(n_layers, W_A, W_A)
    sg_bias = jnp.repeat(jnp.swapaxes(sg_b, 1, 2), SG_HEAD_DIM, axis=2)
    p = {
        "pre_mix_g": row(pre_mix_g), "w_in": w_in, "b_in": row(b_in),
        "pool_w": pool_bd.astype(BF16), "pool_scale": row(pool_scale),
        "sg_ln_g": row(sg_ln_g), "sg_ln_b": row(sg_ln_b),
        "sg_w": sg_w.reshape(n_layers, SG_HEADS * SG_BLOCK, SG_BLOCK), "sg_b": sg_bias,
        "conv_w": conv_w, "conv_b": row(conv_b),
        "conv_ln_g": row(conv_ln_g), "conv_ln_b": row(conv_ln_b),
        "w_out": w_out, "post_mix_g": row(post_mix_g),
        "pre_x_g": row(pre_x_g), "post_x_g": row(post_x_g),
        "pre_ff_g": row(pre_ff_g), "w_ff1": w_ff1, "w_ff2": w_ff2,
        "post_ff_g": row(post_ff_g),
    }
    wqk, vwo = _fold_memory(mem, row(mem_g), wq, wk, wv, wo)
    for layer in range(n_layers):
        x = _mixer(x, layer, p)
        x = _attention(x, layer, p, wqk, vwo)
        x = _ffn(x, layer, p)
    return x
```

```python
import functools
import math

import jax
import jax.numpy as jnp
from jax import lax
from jax.experimental import pallas as pl
from jax.experimental.pallas import tpu as pltpu

EPS = 1e-6
BF16 = jnp.bfloat16
F32 = jnp.float32

POOL_WINDOWS = (2, 4, 8, 16)
POOL_CH = 64
W_A = len(POOL_WINDOWS) * POOL_CH
SG_BLOCK = 128
SG_HEADS = 4
SG_HEAD_DIM = 96
W_B = SG_HEADS * SG_HEAD_DIM
CONV_K = 31
X_HEADS = 4

TM_MIX = 1024
TM_ATT = 512
TM_FF = 1024
LANES = 128
CONV_ROW_STRIDE = 2
FF_CHUNK = 1024
POOL_HALO = 16
CONV_HALO = 32
VMEM_LIMIT = 56 * 1024 * 1024


def _dot(a, b):
    return jnp.dot(a, b, preferred_element_type=F32)


def _rms(xf, g):
    ms = jnp.mean(xf * xf, axis=-1, keepdims=True)
    return xf * lax.rsqrt(ms + EPS) * g


def _layer_norm(xf, g, b):
    mu = jnp.mean(xf, axis=-1, keepdims=True)
    xc = xf - mu
    var = jnp.mean(xc * xc, axis=-1, keepdims=True)
    return xc * lax.rsqrt(var + EPS) * g + b


def _kv_kernel(mem_ref, g_ref, wq_ref, wk_ref, wv_ref, wo_ref, wqk_ref, vwo_ref):
    d_model = mem_ref.shape[-1]
    dh = d_model // X_HEADS
    scale = 1.0 / math.sqrt(dh)
    m = _rms(mem_ref[...], g_ref[...]).astype(BF16)
    k = _dot(m, wk_ref[...].astype(BF16)).astype(BF16)
    v = _dot(m, wv_ref[...].astype(BF16)).astype(BF16)
    for h in range(X_HEADS):
        hs = slice(h * dh, (h + 1) * dh)
        s = lax.dot_general(wq_ref[:, hs].astype(BF16), k[:, hs], (((1,), (1,)), ((), ())),
                            preferred_element_type=F32)
        wqk_ref[:, hs] = (s * scale).astype(BF16)
        vwo_ref[hs, :] = _dot(v[:, hs], wo_ref[hs, :].astype(BF16)).astype(BF16)


def _fold_memory(mem, mem_g, wq, wk, wv, wo):
    n_layers, d_model, _ = wq.shape
    bsz, n_mem, _ = mem.shape
    assert n_mem * X_HEADS == d_model
    w_spec = pl.BlockSpec((None, d_model, d_model), lambda l, b: (l, 0, 0))
    out_spec = pl.BlockSpec((None, None, d_model, d_model), lambda l, b: (l, b, 0, 0))
    out_shape = jax.ShapeDtypeStruct((n_layers, bsz, d_model, d_model), BF16)
    return pl.pallas_call(
        _kv_kernel,
        grid=(n_layers, bsz),
        in_specs=[
            pl.BlockSpec((None, n_mem, d_model), lambda l, b: (b, 0, 0)),
            pl.BlockSpec((None, 1, d_model), lambda l, b: (l, 0, 0)),
            w_spec, w_spec, w_spec, w_spec,
        ],
        out_specs=[out_spec, out_spec],
        out_shape=[out_shape, out_shape],
        compiler_params=pltpu.CompilerParams(
            dimension_semantics=("arbitrary", "arbitrary"),
            vmem_limit_bytes=VMEM_LIMIT),
        name="fold_memory",
    )(mem, mem_g, wq, wk, wv, wo)


def _mixer_kernel(x_ref, g1_ref, win_ref, bin_ref, poolw_ref, pscale_ref,
                  sglng_ref, sglnb_ref, sgw_ref, sgbias_ref,
                  convw_ref, convb_ref, clng_ref, clnb_ref, wout_ref, g2_ref,
                  o_ref, abuf, ycat, hn_ref, z_ref, *slabs):
    tm = x_ref.shape[0]
    n_slab = len(slabs) // 2
    hbufs, cbufs = slabs[:n_slab], slabs[n_slab:]
    w_c = n_slab * LANES
    off_b = W_A
    off_c = W_A + 2 * W_B
    i = pl.program_id(1)

    @pl.when(i == 0)
    def _():
        abuf[0:POOL_HALO, :] = jnp.zeros((POOL_HALO, W_A), F32)
        for hb in hbufs:
            hb[0:CONV_HALO, :] = jnp.zeros((CONV_HALO, LANES), F32)

    hn_ref[...] = _rms(x_ref[...], g1_ref[...]).astype(BF16)
    z_ref[...] = _dot(hn_ref[...], win_ref[...].astype(BF16)) + bin_ref[...]

    abuf[POOL_HALO:POOL_HALO + tm, :] = z_ref[:, 0:W_A]
    rows = 128
    lane = lax.broadcasted_iota(jnp.int32, (rows, W_A), 1)
    row = lax.broadcasted_iota(jnp.int32, (rows, W_A), 0)
    group = lane // POOL_CH
    window = jnp.full((rows, W_A), POOL_WINDOWS[-1], jnp.int32)
    for gi in range(len(POOL_WINDOWS) - 2, -1, -1):
        window = jnp.where(group == gi, POOL_WINDOWS[gi], window)
    for r0 in range(0, tm, rows):
        s = abuf[r0:r0 + POOL_HALO + rows, :]
        a0 = s[POOL_HALO:]
        win_sum = None
        w = 1
        for gi, target in enumerate(POOL_WINDOWS):
            while w < target:
                s = s + pltpu.roll(s, w, 0)
                w *= 2
            cur = s[POOL_HALO:]
            win_sum = cur if win_sum is None else jnp.where(group >= gi, cur, win_sum)
        pos1 = i * tm + r0 + row + 1
        cnt = jnp.minimum(pos1, window).astype(F32)
        p = (win_sum / cnt - a0).astype(BF16)
        ya = _dot(p, poolw_ref[...]) * pscale_ref[...]
        ycat[r0:r0 + rows, 0:W_A] = ya.astype(BF16)
    abuf[0:POOL_HALO, :] = abuf[tm:tm + POOL_HALO, :]

    wrow = lax.broadcasted_iota(jnp.int32, (SG_HEADS * SG_BLOCK, SG_BLOCK), 0)
    wcol = lax.broadcasted_iota(jnp.int32, (SG_HEADS * SG_BLOCK, SG_BLOCK), 1)
    causal = (wrow & (SG_BLOCK - 1)) >= wcol
    wst = jnp.where(causal, sgw_ref[...], 0.0).astype(BF16)
    hl = lax.broadcasted_iota(jnp.int32, (SG_BLOCK, W_B), 1)
    for blk in range(tm // SG_BLOCK):
        rs = slice(blk * SG_BLOCK, (blk + 1) * SG_BLOCK)
        v = _layer_norm(jax.nn.gelu(z_ref[rs, off_b + W_B:off_b + 2 * W_B]),
                        sglng_ref[...], sglnb_ref[...]).astype(BF16)
        res = _dot(wst, v)
        sv = res[(SG_HEADS - 1) * SG_BLOCK:SG_HEADS * SG_BLOCK]
        for h in range(SG_HEADS - 2, -1, -1):
            sv = jnp.where(hl < (h + 1) * SG_HEAD_DIM, res[h * SG_BLOCK:(h + 1) * SG_BLOCK], sv)
        yb = jax.nn.gelu(z_ref[rs, off_b:off_b + W_B]) * (sv + sgbias_ref[...])
        ycat[rs, W_A:W_A + W_B] = yb.astype(BF16)

    grows = 64
    for r0 in range(0, tm, grows):
        rs = slice(r0, r0 + grows)
        glu = z_ref[rs, off_c:off_c + w_c] * jax.nn.sigmoid(z_ref[rs, off_c + w_c:off_c + 2 * w_c])
        for c in range(n_slab):
            hbufs[c][CONV_HALO + r0:CONV_HALO + r0 + grows, :] = glu[:, c * LANES:(c + 1) * LANES]
    lead = CONV_HALO - (CONV_K - 1)
    crows = 256
    nrow = crows // CONV_ROW_STRIDE
    for c in range(n_slab):
        cs = slice(c * LANES, (c + 1) * LANES)
        for r0 in range(0, tm, crows):
            for ph in range(CONV_ROW_STRIDE):
                acc = None
                for k in range(CONV_K):
                    t = (hbufs[c][pl.ds(r0 + ph + lead + k, nrow, stride=CONV_ROW_STRIDE), :]
                         * convw_ref[k:k + 1, cs])
                    acc = t if acc is None else acc + t
                cbufs[c][pl.ds(r0 + ph, nrow, stride=CONV_ROW_STRIDE), :] = acc + convb_ref[:, cs]
    lrows = 128
    for r0 in range(0, tm, lrows):
        conv = jnp.concatenate([cbufs[c][r0:r0 + lrows, :] for c in range(n_slab)], axis=-1)
        hc = _layer_norm(conv, clng_ref[...], clnb_ref[...])
        ycat[r0:r0 + lrows, W_A + W_B:W_A + W_B + w_c] = jax.nn.silu(hc).astype(BF16)
    for c in range(n_slab):
        hbufs[c][0:CONV_HALO, :] = hbufs[c][tm:tm + CONV_HALO, :]

    y = _dot(ycat[...], wout_ref[...].astype(BF16))
    o_ref[...] = x_ref[...] + _rms(y, g2_ref[...])


def _const_spec(shape, layer):
    nd = len(shape)
    return pl.BlockSpec((None,) + tuple(shape), lambda b, i: (layer,) + (0,) * nd,
                        pipeline_mode=pl.Buffered(1))


def _mixer(x, layer, p):
    bsz, seq, d_model = x.shape
    tm = TM_MIX
    assert seq % tm == 0 and tm % SG_BLOCK == 0
    d_in = p["w_in"].shape[-1]
    w_c = (d_in - W_A - 2 * W_B) // 2
    assert W_A + W_B + w_c == d_model and w_c % LANES == 0
    n_slab = w_c // LANES
    assert CONV_HALO >= CONV_K - 1 and tm % 256 == 0
    x_spec = pl.BlockSpec((None, tm, d_model), lambda b, i: (b, i, 0))
    cs = functools.partial(_const_spec, layer=layer)
    return pl.pallas_call(
        _mixer_kernel,
        grid=(bsz, seq // tm),
        in_specs=[
            x_spec,
            cs((1, d_model)),
            cs((d_model, d_in)),
            cs((1, d_in)),
            cs((W_A, W_A)),
            cs((1, W_A)),
            cs((1, W_B)), cs((1, W_B)),
            cs((SG_HEADS * SG_BLOCK, SG_BLOCK)),
            cs((SG_BLOCK, W_B)),
            cs((CONV_K, w_c)), cs((1, w_c)),
            cs((1, w_c)), cs((1, w_c)),
            cs((d_model, d_model)),
            cs((1, d_model)),
        ],
        out_specs=x_spec,
        out_shape=jax.ShapeDtypeStruct(x.shape, x.dtype),
        scratch_shapes=[
            pltpu.VMEM((tm + POOL_HALO, W_A), F32),
            pltpu.VMEM((tm, d_model), BF16),
            pltpu.VMEM((tm, d_model), BF16),
            pltpu.VMEM((tm, d_in), F32),
        ] + [pltpu.VMEM((tm + CONV_HALO, LANES), F32)] * n_slab
          + [pltpu.VMEM((tm, LANES), F32)] * n_slab,
        compiler_params=pltpu.CompilerParams(
            dimension_semantics=("arbitrary", "arbitrary"),
            vmem_limit_bytes=VMEM_LIMIT),
        name=f"mixer_l{layer}",
    )(x, p["pre_mix_g"], p["w_in"], p["b_in"], p["pool_w"], p["pool_scale"],
      p["sg_ln_g"], p["sg_ln_b"], p["sg_w"], p["sg_b"],
      p["conv_w"], p["conv_b"], p["conv_ln_g"], p["conv_ln_b"], p["w_out"], p["post_mix_g"])


def _attn_kernel(x_ref, g1_ref, wqk_ref, vwo_ref, g2_ref, o_ref, hn_ref, s_ref, p_ref, acc_ref):
    tm, d_model = x_ref.shape
    n_mem = d_model // X_HEADS
    hn_ref[...] = _rms(x_ref[...], g1_ref[...]).astype(BF16)
    s_ref[...] = _dot(hn_ref[...], wqk_ref[...])
    rows = 128
    for h in range(X_HEADS):
        hs = slice(h * n_mem, (h + 1) * n_mem)
        for r0 in range(0, tm, rows):
            s = s_ref[r0:r0 + rows, hs]
            e = jnp.exp(s - jnp.max(s, axis=-1, keepdims=True))
            p_ref[r0:r0 + rows, :] = (e / jnp.sum(e, axis=-1, keepdims=True)).astype(BF16)
        part = _dot(p_ref[...], vwo_ref[hs, :])
        if h == 0:
            acc_ref[...] = part
        else:
            acc_ref[...] += part
    o_ref[...] = x_ref[...] + _rms(acc_ref[...], g2_ref[...])


def _attention(x, layer, p, wqk, vwo):
    bsz, seq, d_model = x.shape
    tm = TM_ATT
    assert seq % tm == 0
    x_spec = pl.BlockSpec((None, tm, d_model), lambda b, i: (b, i, 0))
    kv_spec = pl.BlockSpec((None, None, d_model, d_model), lambda b, i: (layer, b, 0, 0))
    cs = functools.partial(_const_spec, layer=layer)
    return pl.pallas_call(
        _attn_kernel,
        grid=(bsz, seq // tm),
        in_specs=[x_spec, cs((1, d_model)), kv_spec, kv_spec, cs((1, d_model))],
        out_specs=x_spec,
        out_shape=jax.ShapeDtypeStruct(x.shape, x.dtype),
        scratch_shapes=[pltpu.VMEM((tm, d_model), BF16),
                        pltpu.VMEM((tm, d_model), F32),
                        pltpu.VMEM((tm, d_model // X_HEADS), BF16),
                        pltpu.VMEM((tm, d_model), F32)],
        compiler_params=pltpu.CompilerParams(
            dimension_semantics=("arbitrary", "arbitrary"),
            vmem_limit_bytes=VMEM_LIMIT),
        name=f"attn_l{layer}",
    )(x, p["pre_x_g"], wqk, vwo, p["post_x_g"])


def _ffn_kernel(x_ref, g1_ref, w1_ref, w2_ref, g2_ref, o_ref, hn_ref, hc_ref):
    d_ff = w1_ref.shape[1]
    hn_ref[...] = _rms(x_ref[...], g1_ref[...]).astype(BF16)
    for c in range(d_ff // FF_CHUNK):
        cs = slice(c * FF_CHUNK, (c + 1) * FF_CHUNK)
        a = jnp.maximum(_dot(hn_ref[...], w1_ref[:, cs].astype(BF16)), 0.0)
        hc_ref[...] = (a * a).astype(BF16)
        part = _dot(hc_ref[...], w2_ref[cs, :].astype(BF16))
        if c == 0:
            o_ref[...] = part
        else:
            o_ref[...] += part
    o_ref[...] = x_ref[...] + _rms(o_ref[...], g2_ref[...])


def _ffn(x, layer, p):
    bsz, seq, d_model = x.shape
    d_ff = p["w_ff1"].shape[-1]
    tm = TM_FF
    assert seq % tm == 0 and d_ff % FF_CHUNK == 0
    x_spec = pl.BlockSpec((None, tm, d_model), lambda b, i: (b, i, 0))
    cs = functools.partial(_const_spec, layer=layer)
    return pl.pallas_call(
        _ffn_kernel,
        grid=(bsz, seq // tm),
        in_specs=[x_spec, cs((1, d_model)), cs((d_model, d_ff)), cs((d_ff, d_model)),
                  cs((1, d_model))],
        out_specs=x_spec,
        out_shape=jax.ShapeDtypeStruct(x.shape, x.dtype),
        scratch_shapes=[pltpu.VMEM((tm, d_model), BF16), pltpu.VMEM((tm, FF_CHUNK), BF16)],
        compiler_params=pltpu.CompilerParams(
            dimension_semantics=("arbitrary", "arbitrary"),
            vmem_limit_bytes=VMEM_LIMIT),
        name=f"ffn_l{layer}",
    )(x, p["pre_ff_g"], p["w_ff1"], p["w_ff2"], p["post_ff_g"])


def kernel(x, mem, pre_mix_g, w_in, b_in, pool_w, pool_scale, sg_ln_g, sg_ln_b, sg_w, sg_b, conv_w, conv_b, conv_ln_g, conv_ln_b, w_out, post_mix_g, pre_x_g, mem_g, wq, wk, wv, wo, post_x_g, pre_ff_g, w_ff1, w_ff2, post_ff_g):
    n_layers = w_in.shape[0]
    n_groups = pool_w.shape[1]

    def row(v):
        return v[:, None, :]

    eye = jnp.eye(n_groups, dtype=pool_w.dtype)
    pool_bd = jnp.einsum("lgcd,gh->lgchd", pool_w, eye).reshape(n_layers, W_A, W_A)
    sg_bias = jnp.repeat(jnp.swapaxes(sg_b, 1, 2), SG_HEAD_DIM, axis=2)
    p = {
        "pre_mix_g": row(pre_mix_g), "w_in": w_in, "b_in": row(b_in),
        "pool_w": pool_bd.astype(BF16), "pool_scale": row(pool_scale),
        "sg_ln_g": row(sg_ln_g), "sg_ln_b": row(sg_ln_b),
        "sg_w": sg_w.reshape(n_layers, SG_HEADS * SG_BLOCK, SG_BLOCK), "sg_b": sg_bias,
        "conv_w": conv_w, "conv_b": row(conv_b),
        "conv_ln_g": row(conv_ln_g), "conv_ln_b": row(conv_ln_b),
        "w_out": w_out, "post_mix_g": row(post_mix_g),
        "pre_x_g": row(pre_x_g), "post_x_g": row(post_x_g),
        "pre_ff_g": row(pre_ff_g), "w_ff1": w_ff1, "w_ff2": w_ff2,
        "post_ff_g": row(post_ff_g),
    }
    wqk, vwo = _fold_memory(mem, row(mem_g), wq, wk, wv, wo)
    for layer in range(n_layers):
        x = _mixer(x, layer, p)
        x = _attention(x, layer, p, wqk, vwo)
        x = _ffn(x, layer, p)
    return x
```

```python
import functools
import math

import jax
import jax.numpy as jnp
from jax import lax
from jax.experimental import pallas as pl
from jax.experimental.pallas import tpu as pltpu

EPS = 1e-6
BF16 = jnp.bfloat16
F32 = jnp.float32

POOL_WINDOWS = (2, 4, 8, 16)
POOL_CH = 64
W_A = len(POOL_WINDOWS) * POOL_CH
SG_BLOCK = 128
SG_HEADS = 4
SG_HEAD_DIM = 96
W_B = SG_HEADS * SG_HEAD_DIM
CONV_K = 31
X_HEADS = 4

TM_MIX = 1024
TM_ATT = 512
TM_FF = 1024
LANES = 128
CONV_ROW_STRIDE = 2
STRIDED_ROW_LIMIT = 2048
FF_CHUNK = 1024
POOL_HALO = 16
CONV_HALO = 32
VMEM_LIMIT = 56 * 1024 * 1024


def _dot(a, b):
    return jnp.dot(a, b, preferred_element_type=F32)


def _rms(xf, g):
    ms = jnp.mean(xf * xf, axis=-1, keepdims=True)
    return xf * lax.rsqrt(ms + EPS) * g


def _layer_norm(xf, g, b):
    mu = jnp.mean(xf, axis=-1, keepdims=True)
    xc = xf - mu
    var = jnp.mean(xc * xc, axis=-1, keepdims=True)
    return xc * lax.rsqrt(var + EPS) * g + b


def _kv_kernel(mem_ref, g_ref, wq_ref, wk_ref, wv_ref, wo_ref, wqk_ref, vwo_ref):
    d_model = mem_ref.shape[-1]
    dh = d_model // X_HEADS
    scale = 1.0 / math.sqrt(dh)
    m = _rms(mem_ref[...], g_ref[...]).astype(BF16)
    k = _dot(m, wk_ref[...].astype(BF16)).astype(BF16)
    v = _dot(m, wv_ref[...].astype(BF16)).astype(BF16)
    for h in range(X_HEADS):
        hs = slice(h * dh, (h + 1) * dh)
        s = lax.dot_general(wq_ref[:, hs].astype(BF16), k[:, hs], (((1,), (1,)), ((), ())),
                            preferred_element_type=F32)
        wqk_ref[:, hs] = (s * scale).astype(BF16)
        vwo_ref[hs, :] = _dot(v[:, hs], wo_ref[hs, :].astype(BF16)).astype(BF16)


def _fold_memory(mem, mem_g, wq, wk, wv, wo):
    n_layers, d_model, _ = wq.shape
    bsz, n_mem, _ = mem.shape
    assert n_mem * X_HEADS == d_model
    w_spec = pl.BlockSpec((None, d_model, d_model), lambda l, b: (l, 0, 0))
    out_spec = pl.BlockSpec((None, None, d_model, d_model), lambda l, b: (l, b, 0, 0))
    out_shape = jax.ShapeDtypeStruct((n_layers, bsz, d_model, d_model), BF16)
    return pl.pallas_call(
        _kv_kernel,
        grid=(n_layers, bsz),
        in_specs=[
            pl.BlockSpec((None, n_mem, d_model), lambda l, b: (b, 0, 0)),
            pl.BlockSpec((None, 1, d_model), lambda l, b: (l, 0, 0)),
            w_spec, w_spec, w_spec, w_spec,
        ],
        out_specs=[out_spec, out_spec],
        out_shape=[out_shape, out_shape],
        compiler_params=pltpu.CompilerParams(
            dimension_semantics=("arbitrary", "arbitrary"),
            vmem_limit_bytes=VMEM_LIMIT),
        name="fold_memory",
    )(mem, mem_g, wq, wk, wv, wo)


def _mixer_kernel(x_ref, g1_ref, win_ref, bin_ref, poolw_ref, pscale_ref,
                  sglng_ref, sglnb_ref, sgw_ref, sgbias_ref,
                  convw_ref, convb_ref, clng_ref, clnb_ref, wout_ref, g2_ref,
                  o_ref, abuf, ycat, hn_ref, z_ref, *slabs):
    tm = x_ref.shape[0]
    n_slab = len(slabs) // 2
    hbufs, cbufs = slabs[:n_slab], slabs[n_slab:]
    w_c = n_slab * LANES
    off_b = W_A
    off_c = W_A + 2 * W_B
    i = pl.program_id(1)

    @pl.when(i == 0)
    def _():
        abuf[0:POOL_HALO, :] = jnp.zeros((POOL_HALO, W_A), F32)
        for hb in hbufs:
            hb[0:CONV_HALO, :] = jnp.zeros((CONV_HALO, LANES), F32)

    hn_ref[...] = _rms(x_ref[...], g1_ref[...]).astype(BF16)
    z_ref[...] = _dot(hn_ref[...], win_ref[...].astype(BF16)) + bin_ref[...]

    abuf[POOL_HALO:POOL_HALO + tm, :] = z_ref[:, 0:W_A]
    rows = 128
    lane = lax.broadcasted_iota(jnp.int32, (rows, W_A), 1)
    row = lax.broadcasted_iota(jnp.int32, (rows, W_A), 0)
    group = lane // POOL_CH
    window = jnp.full((rows, W_A), POOL_WINDOWS[-1], jnp.int32)
    for gi in range(len(POOL_WINDOWS) - 2, -1, -1):
        window = jnp.where(group == gi, POOL_WINDOWS[gi], window)
    for r0 in range(0, tm, rows):
        s = abuf[r0:r0 + POOL_HALO + rows, :]
        a0 = s[POOL_HALO:]
        win_sum = None
        w = 1
        for gi, target in enumerate(POOL_WINDOWS):
            while w < target:
                s = s + pltpu.roll(s, w, 0)
                w *= 2
            cur = s[POOL_HALO:]
            win_sum = cur if win_sum is None else jnp.where(group >= gi, cur, win_sum)
        pos1 = i * tm + r0 + row + 1
        cnt = jnp.minimum(pos1, window).astype(F32)
        p = (win_sum / cnt - a0).astype(BF16)
        ya = _dot(p, poolw_ref[...]) * pscale_ref[...]
        ycat[r0:r0 + rows, 0:W_A] = ya.astype(BF16)
    abuf[0:POOL_HALO, :] = abuf[tm:tm + POOL_HALO, :]

    wrow = lax.broadcasted_iota(jnp.int32, (SG_HEADS * SG_BLOCK, SG_BLOCK), 0)
    wcol = lax.broadcasted_iota(jnp.int32, (SG_HEADS * SG_BLOCK, SG_BLOCK), 1)
    causal = (wrow & (SG_BLOCK - 1)) >= wcol
    wst = jnp.where(causal, sgw_ref[...], 0.0).astype(BF16)
    hl = lax.broadcasted_iota(jnp.int32, (SG_BLOCK, W_B), 1)
    for blk in range(tm // SG_BLOCK):
        rs = slice(blk * SG_BLOCK, (blk + 1) * SG_BLOCK)
        v = _layer_norm(jax.nn.gelu(z_ref[rs, off_b + W_B:off_b + 2 * W_B]),
                        sglng_ref[...], sglnb_ref[...]).astype(BF16)
        res = _dot(wst, v)
        sv = res[(SG_HEADS - 1) * SG_BLOCK:SG_HEADS * SG_BLOCK]
        for h in range(SG_HEADS - 2, -1, -1):
            sv = jnp.where(hl < (h + 1) * SG_HEAD_DIM, res[h * SG_BLOCK:(h + 1) * SG_BLOCK], sv)
        yb = jax.nn.gelu(z_ref[rs, off_b:off_b + W_B]) * (sv + sgbias_ref[...])
        ycat[rs, W_A:W_A + W_B] = yb.astype(BF16)

    grows = 64
    for r0 in range(0, tm, grows):
        rs = slice(r0, r0 + grows)
        glu = z_ref[rs, off_c:off_c + w_c] * jax.nn.sigmoid(z_ref[rs, off_c + w_c:off_c + 2 * w_c])
        for c in range(n_slab):
            hbufs[c][CONV_HALO + r0:CONV_HALO + r0 + grows, :] = glu[:, c * LANES:(c + 1) * LANES]
    lead = CONV_HALO - (CONV_K - 1)
    crows = 256
    nrow = crows // CONV_ROW_STRIDE
    for c in range(n_slab):
        cs = slice(c * LANES, (c + 1) * LANES)
        for r0 in range(0, tm, crows):
            for ph in range(CONV_ROW_STRIDE):
                acc = None
                for k in range(CONV_K):
                    t = (hbufs[c][pl.ds(r0 + ph + lead + k, nrow, stride=CONV_ROW_STRIDE), :]
                         * convw_ref[k:k + 1, cs])
                    acc = t if acc is None else acc + t
                cbufs[c][pl.ds(r0 + ph, nrow, stride=CONV_ROW_STRIDE), :] = acc + convb_ref[:, cs]
    lrows = 128
    for r0 in range(0, tm, lrows):
        conv = jnp.concatenate([cbufs[c][r0:r0 + lrows, :] for c in range(n_slab)], axis=-1)
        hc = _layer_norm(conv, clng_ref[...], clnb_ref[...])
        ycat[r0:r0 + lrows, W_A + W_B:W_A + W_B + w_c] = jax.nn.silu(hc).astype(BF16)
    for c in range(n_slab):
        hbufs[c][0:CONV_HALO, :] = hbufs[c][tm:tm + CONV_HALO, :]

    y = _dot(ycat[...], wout_ref[...].astype(BF16))
    o_ref[...] = x_ref[...] + _rms(y, g2_ref[...])


def _const_spec(shape, layer):
    nd = len(shape)
    return pl.BlockSpec((None,) + tuple(shape), lambda b, i: (layer,) + (0,) * nd,
                        pipeline_mode=pl.Buffered(1))


def _mixer(x, layer, p):
    bsz, seq, d_model = x.shape
    tm = TM_MIX
    assert seq % tm == 0 and tm % SG_BLOCK == 0
    d_in = p["w_in"].shape[-1]
    w_c = (d_in - W_A - 2 * W_B) // 2
    assert W_A + W_B + w_c == d_model and w_c % LANES == 0
    n_slab = w_c // LANES
    assert CONV_HALO >= CONV_K - 1 and tm % 256 == 0
    assert tm + CONV_HALO <= STRIDED_ROW_LIMIT
    x_spec = pl.BlockSpec((None, tm, d_model), lambda b, i: (b, i, 0))
    cs = functools.partial(_const_spec, layer=layer)
    return pl.pallas_call(
        _mixer_kernel,
        grid=(bsz, seq // tm),
        in_specs=[
            x_spec,
            cs((1, d_model)),
            cs((d_model, d_in)),
            cs((1, d_in)),
            cs((W_A, W_A)),
            cs((1, W_A)),
            cs((1, W_B)), cs((1, W_B)),
            cs((SG_HEADS * SG_BLOCK, SG_BLOCK)),
            cs((SG_BLOCK, W_B)),
            cs((CONV_K, w_c)), cs((1, w_c)),
            cs((1, w_c)), cs((1, w_c)),
            cs((d_model, d_model)),
            cs((1, d_model)),
        ],
        out_specs=x_spec,
        out_shape=jax.ShapeDtypeStruct(x.shape, x.dtype),
        scratch_shapes=[
            pltpu.VMEM((tm + POOL_HALO, W_A), F32),
            pltpu.VMEM((tm, d_model), BF16),
            pltpu.VMEM((tm, d_model), BF16),
            pltpu.VMEM((tm, d_in), F32),
        ] + [pltpu.VMEM((tm + CONV_HALO, LANES), F32)] * n_slab
          + [pltpu.VMEM((tm, LANES), F32)] * n_slab,
        compiler_params=pltpu.CompilerParams(
            dimension_semantics=("arbitrary", "arbitrary"),
            vmem_limit_bytes=VMEM_LIMIT),
        name=f"mixer_l{layer}",
    )(x, p["pre_mix_g"], p["w_in"], p["b_in"], p["pool_w"], p["pool_scale"],
      p["sg_ln_g"], p["sg_ln_b"], p["sg_w"], p["sg_b"],
      p["conv_w"], p["conv_b"], p["conv_ln_g"], p["conv_ln_b"], p["w_out"], p["post_mix_g"])


def _attn_kernel(x_ref, g1_ref, wqk_ref, vwo_ref, g2_ref, o_ref, hn_ref, s_ref, p_ref, acc_ref):
    tm, d_model = x_ref.shape
    n_mem = d_model // X_HEADS
    hn_ref[...] = _rms(x_ref[...], g1_ref[...]).astype(BF16)
    s_ref[...] = _dot(hn_ref[...], wqk_ref[...])
    rows = 128
    for h in range(X_HEADS):
        hs = slice(h * n_mem, (h + 1) * n_mem)
        for r0 in range(0, tm, rows):
            s = s_ref[r0:r0 + rows, hs]
            e = jnp.exp(s - jnp.max(s, axis=-1, keepdims=True))
            p_ref[r0:r0 + rows, :] = (e / jnp.sum(e, axis=-1, keepdims=True)).astype(BF16)
        part = _dot(p_ref[...], vwo_ref[hs, :])
        if h == 0:
            acc_ref[...] = part
        else:
            acc_ref[...] += part
    o_ref[...] = x_ref[...] + _rms(acc_ref[...], g2_ref[...])


def _attention(x, layer, p, wqk, vwo):
    bsz, seq, d_model = x.shape
    tm = TM_ATT
    assert seq % tm == 0
    x_spec = pl.BlockSpec((None, tm, d_model), lambda b, i: (b, i, 0))
    kv_spec = pl.BlockSpec((None, None, d_model, d_model), lambda b, i: (layer, b, 0, 0))
    cs = functools.partial(_const_spec, layer=layer)
    return pl.pallas_call(
        _attn_kernel,
        grid=(bsz, seq // tm),
        in_specs=[x_spec, cs((1, d_model)), kv_spec, kv_spec, cs((1, d_model))],
        out_specs=x_spec,
        out_shape=jax.ShapeDtypeStruct(x.shape, x.dtype),
        scratch_shapes=[pltpu.VMEM((tm, d_model), BF16),
                        pltpu.VMEM((tm, d_model), F32),
                        pltpu.VMEM((tm, d_model // X_HEADS), BF16),
                        pltpu.VMEM((tm, d_model), F32)],
        compiler_params=pltpu.CompilerParams(
            dimension_semantics=("arbitrary", "arbitrary"),
            vmem_limit_bytes=VMEM_LIMIT),
        name=f"attn_l{layer}",
    )(x, p["pre_x_g"], wqk, vwo, p["post_x_g"])


def _ffn_kernel(x_ref, g1_ref, w1_ref, w2_ref, g2_ref, o_ref, hn_ref, hc_ref):
    d_ff = w1_ref.shape[1]
    hn_ref[...] = _rms(x_ref[...], g1_ref[...]).astype(BF16)
    for c in range(d_ff // FF_CHUNK):
        cs = slice(c * FF_CHUNK, (c + 1) * FF_CHUNK)
        a = jnp.maximum(_dot(hn_ref[...], w1_ref[:, cs].astype(BF16)), 0.0)
        hc_ref[...] = (a * a).astype(BF16)
        part = _dot(hc_ref[...], w2_ref[cs, :].astype(BF16))
        if c == 0:
            o_ref[...] = part
        else:
            o_ref[...] += part
    o_ref[...] = x_ref[...] + _rms(o_ref[...], g2_ref[...])


def _ffn(x, layer, p):
    bsz, seq, d_model = x.shape
    d_ff = p["w_ff1"].shape[-1]
    tm = TM_FF
    assert seq % tm == 0 and d_ff % FF_CHUNK == 0
    x_spec = pl.BlockSpec((None, tm, d_model), lambda b, i: (b, i, 0))
    cs = functools.partial(_const_spec, layer=layer)
    return pl.pallas_call(
        _ffn_kernel,
        grid=(bsz, seq // tm),
        in_specs=[x_spec, cs((1, d_model)), cs((d_model, d_ff)), cs((d_ff, d_model)),
                  cs((1, d_model))],
        out_specs=x_spec,
        out_shape=jax.ShapeDtypeStruct(x.shape, x.dtype),
        scratch_shapes=[pltpu.VMEM((tm, d_model), BF16), pltpu.VMEM((tm, FF_CHUNK), BF16)],
        compiler_params=pltpu.CompilerParams(
            dimension_semantics=("arbitrary", "arbitrary"),
            vmem_limit_bytes=VMEM_LIMIT),
        name=f"ffn_l{layer}",
    )(x, p["pre_ff_g"], p["w_ff1"], p["w_ff2"], p["post_ff_g"])


def kernel(x, mem, pre_mix_g, w_in, b_in, pool_w, pool_scale, sg_ln_g, sg_ln_b, sg_w, sg_b, conv_w, conv_b, conv_ln_g, conv_ln_b, w_out, post_mix_g, pre_x_g, mem_g, wq, wk, wv, wo, post_x_g, pre_ff_g, w_ff1, w_ff2, post_ff_g):
    n_layers = w_in.shape[0]
    n_groups = pool_w.shape[1]

    def row(v):
        return v[:, None, :]

    eye = jnp.eye(n_groups, dtype=pool_w.dtype)
    pool_bd = jnp.einsum("lgcd,gh->lgchd", pool_w, eye).reshape(n_layers, W_A, W_A)
    sg_bias = jnp.repeat(jnp.swapaxes(sg_b, 1, 2), SG_HEAD_DIM, axis=2)
    p = {
        "pre_mix_g": row(pre_mix_g), "w_in": w_in, "b_in": row(b_in),
        "pool_w": pool_bd.astype(BF16), "pool_scale": row(pool_scale),
        "sg_ln_g": row(sg_ln_g), "sg_ln_b": row(sg_ln_b),
        "sg_w": sg_w.reshape(n_layers, SG_HEADS * SG_BLOCK, SG_BLOCK), "sg_b": sg_bias,
        "conv_w": conv_w, "conv_b": row(conv_b),
        "conv_ln_g": row(conv_ln_g), "conv_ln_b": row(conv_ln_b),
        "w_out": w_out, "post_mix_g": row(post_mix_g),
        "pre_x_g": row(pre_x_g), "post_x_g": row(post_x_g),
        "pre_ff_g": row(pre_ff_g), "w_ff1": w_ff1, "w_ff2": w_ff2,
        "post_ff_g": row(post_ff_g),
    }
    wqk, vwo = _fold_memory(mem, row(mem_g), wq, wk, wv, wo)
    for layer in range(n_layers):
        x = _mixer(x, layer, p)
        x = _attention(x, layer, p, wqk, vwo)
        x = _ffn(x, layer, p)
    return x
```

```python
import functools
import math

import jax
import jax.numpy as jnp
from jax import lax
from jax.experimental import pallas as pl
from jax.experimental.pallas import tpu as pltpu

EPS = 1e-6
BF16 = jnp.bfloat16
F32 = jnp.float32

POOL_WINDOWS = (2, 4, 8, 16)
POOL_CH = 64
W_A = len(POOL_WINDOWS) * POOL_CH
SG_BLOCK = 128
SG_HEADS = 4
SG_HEAD_DIM = 96
W_B = SG_HEADS * SG_HEAD_DIM
CONV_K = 31
X_HEADS = 4

TM_MIX = 1024
TM_ATT = 512
TM_FF = 1024
LANES = 128
CONV_ROW_STRIDE = 2
STRIDED_ROW_LIMIT = 2048
FF_CHUNK = 1024
POOL_HALO = 16
CONV_HALO = 32
VMEM_LIMIT = 56 * 1024 * 1024


def _dot(a, b):
    return jnp.dot(a, b, preferred_element_type=F32)


def _rms(xf, g):
    ms = jnp.mean(xf * xf, axis=-1, keepdims=True)
    return xf * lax.rsqrt(ms + EPS) * g


def _layer_norm(xf, g, b):
    mu = jnp.mean(xf, axis=-1, keepdims=True)
    xc = xf - mu
    var = jnp.mean(xc * xc, axis=-1, keepdims=True)
    return xc * lax.rsqrt(var + EPS) * g + b


def _kv_kernel(mem_ref, g_ref, wq_ref, wk_ref, wv_ref, wo_ref, wqk_ref, vwo_ref):
    bsz, n_mem, d_model = mem_ref.shape
    dh = d_model // X_HEADS
    scale = 1.0 / math.sqrt(dh)
    m = _rms(mem_ref[...].reshape(bsz * n_mem, d_model), g_ref[...]).astype(BF16)
    k = _dot(m, wk_ref[...].astype(BF16)).astype(BF16)
    v = _dot(m, wv_ref[...].astype(BF16)).astype(BF16)
    for h in range(X_HEADS):
        hs = slice(h * dh, (h + 1) * dh)
        wq_h = wq_ref[:, hs].astype(BF16)
        wo_h = wo_ref[hs, :].astype(BF16)
        for b in range(bsz):
            bs = slice(b * n_mem, (b + 1) * n_mem)
            s = lax.dot_general(wq_h, k[bs, hs], (((1,), (1,)), ((), ())),
                                preferred_element_type=F32)
            wqk_ref[b, :, hs] = (s * scale).astype(BF16)
            vwo_ref[b, hs, :] = _dot(v[bs, hs], wo_h).astype(BF16)


def _fold_memory(mem, mem_g, wq, wk, wv, wo):
    n_layers, d_model, _ = wq.shape
    bsz, n_mem, _ = mem.shape
    assert n_mem * X_HEADS == d_model
    w_spec = pl.BlockSpec((None, d_model, d_model), lambda l: (l, 0, 0))
    out_spec = pl.BlockSpec((None, bsz, d_model, d_model), lambda l: (l, 0, 0, 0))
    out_shape = jax.ShapeDtypeStruct((n_layers, bsz, d_model, d_model), BF16)
    return pl.pallas_call(
        _kv_kernel,
        grid=(n_layers,),
        in_specs=[
            pl.BlockSpec((bsz, n_mem, d_model), lambda l: (0, 0, 0)),
            pl.BlockSpec((None, 1, d_model), lambda l: (l, 0, 0)),
            w_spec, w_spec, w_spec, w_spec,
        ],
        out_specs=[out_spec, out_spec],
        out_shape=[out_shape, out_shape],
        compiler_params=pltpu.CompilerParams(
            dimension_semantics=("arbitrary",),
            vmem_limit_bytes=VMEM_LIMIT),
        name="fold_memory",
    )(mem, mem_g, wq, wk, wv, wo)


def _mixer_kernel(x_ref, g1_ref, win_ref, bin_ref, poolw_ref, pscale_ref,
                  sglng_ref, sglnb_ref, sgw_ref, sgbias_ref,
                  convw_ref, convb_ref, clng_ref, clnb_ref, wout_ref, g2_ref,
                  o_ref, abuf, ycat, hn_ref, z_ref, *slabs):
    tm = x_ref.shape[0]
    n_slab = len(slabs) // 2
    hbufs, cbufs = slabs[:n_slab], slabs[n_slab:]
    w_c = n_slab * LANES
    off_b = W_A
    off_c = W_A + 2 * W_B
    i = pl.program_id(1)

    @pl.when(i == 0)
    def _():
        abuf[0:POOL_HALO, :] = jnp.zeros((POOL_HALO, W_A), F32)
        for hb in hbufs:
            hb[0:CONV_HALO, :] = jnp.zeros((CONV_HALO, LANES), F32)

    hn_ref[...] = _rms(x_ref[...], g1_ref[...]).astype(BF16)
    z_ref[...] = _dot(hn_ref[...], win_ref[...].astype(BF16)) + bin_ref[...]

    abuf[POOL_HALO:POOL_HALO + tm, :] = z_ref[:, 0:W_A]
    rows = 128
    lane = lax.broadcasted_iota(jnp.int32, (rows, W_A), 1)
    row = lax.broadcasted_iota(jnp.int32, (rows, W_A), 0)
    group = lane // POOL_CH
    window = jnp.full((rows, W_A), POOL_WINDOWS[-1], jnp.int32)
    for gi in range(len(POOL_WINDOWS) - 2, -1, -1):
        window = jnp.where(group == gi, POOL_WINDOWS[gi], window)
    for r0 in range(0, tm, rows):
        s = abuf[r0:r0 + POOL_HALO + rows, :]
        a0 = s[POOL_HALO:]
        win_sum = None
        w = 1
        for gi, target in enumerate(POOL_WINDOWS):
            while w < target:
                s = s + pltpu.roll(s, w, 0)
                w *= 2
            cur = s[POOL_HALO:]
            win_sum = cur if win_sum is None else jnp.where(group >= gi, cur, win_sum)
        pos1 = i * tm + r0 + row + 1
        cnt = jnp.minimum(pos1, window).astype(F32)
        p = (win_sum / cnt - a0).astype(BF16)
        ya = _dot(p, poolw_ref[...]) * pscale_ref[...]
        ycat[r0:r0 + rows, 0:W_A] = ya.astype(BF16)
    abuf[0:POOL_HALO, :] = abuf[tm:tm + POOL_HALO, :]

    wrow = lax.broadcasted_iota(jnp.int32, (SG_HEADS * SG_BLOCK, SG_BLOCK), 0)
    wcol = lax.broadcasted_iota(jnp.int32, (SG_HEADS * SG_BLOCK, SG_BLOCK), 1)
    causal = (wrow & (SG_BLOCK - 1)) >= wcol
    wst = jnp.where(causal, sgw_ref[...], 0.0).astype(BF16)
    hl = lax.broadcasted_iota(jnp.int32, (SG_BLOCK, W_B), 1)
    for blk in range(tm // SG_BLOCK):
        rs = slice(blk * SG_BLOCK, (blk + 1) * SG_BLOCK)
        v = _layer_norm(jax.nn.gelu(z_ref[rs, off_b + W_B:off_b + 2 * W_B]),
                        sglng_ref[...], sglnb_ref[...]).astype(BF16)
        res = _dot(wst, v)
        sv = res[(SG_HEADS - 1) * SG_BLOCK:SG_HEADS * SG_BLOCK]
        for h in range(SG_HEADS - 2, -1, -1):
            sv = jnp.where(hl < (h + 1) * SG_HEAD_DIM, res[h * SG_BLOCK:(h + 1) * SG_BLOCK], sv)
        yb = jax.nn.gelu(z_ref[rs, off_b:off_b + W_B]) * (sv + sgbias_ref[...])
        ycat[rs, W_A:W_A + W_B] = yb.astype(BF16)

    grows = 64
    for r0 in range(0, tm, grows):
        rs = slice(r0, r0 + grows)
        glu = z_ref[rs, off_c:off_c + w_c] * jax.nn.sigmoid(z_ref[rs, off_c + w_c:off_c + 2 * w_c])
        for c in range(n_slab):
            hbufs[c][CONV_HALO + r0:CONV_HALO + r0 + grows, :] = glu[:, c * LANES:(c + 1) * LANES]
    lead = CONV_HALO - (CONV_K - 1)
    crows = 256
    nrow = crows // CONV_ROW_STRIDE
    for c in range(n_slab):
        cs = slice(c * LANES, (c + 1) * LANES)
        for r0 in range(0, tm, crows):
            for ph in range(CONV_ROW_STRIDE):
                acc = None
                for k in range(CONV_K):
                    t = (hbufs[c][pl.ds(r0 + ph + lead + k, nrow, stride=CONV_ROW_STRIDE), :]
                         * convw_ref[k:k + 1, cs])
                    acc = t if acc is None else acc + t
                cbufs[c][pl.ds(r0 + ph, nrow, stride=CONV_ROW_STRIDE), :] = acc + convb_ref[:, cs]
    lrows = 128
    for r0 in range(0, tm, lrows):
        conv = jnp.concatenate([cbufs[c][r0:r0 + lrows, :] for c in range(n_slab)], axis=-1)
        hc = _layer_norm(conv, clng_ref[...], clnb_ref[...])
        ycat[r0:r0 + lrows, W_A + W_B:W_A + W_B + w_c] = jax.nn.silu(hc).astype(BF16)
    for c in range(n_slab):
        hbufs[c][0:CONV_HALO, :] = hbufs[c][tm:tm + CONV_HALO, :]

    y = _dot(ycat[...], wout_ref[...].astype(BF16))
    o_ref[...] = x_ref[...] + _rms(y, g2_ref[...])


def _const_spec(shape, layer):
    nd = len(shape)
    return pl.BlockSpec((None,) + tuple(shape), lambda b, i: (layer,) + (0,) * nd,
                        pipeline_mode=pl.Buffered(1))


def _mixer(x, layer, p):
    bsz, seq, d_model = x.shape
    tm = TM_MIX
    assert seq % tm == 0 and tm % SG_BLOCK == 0
    d_in = p["w_in"].shape[-1]
    w_c = (d_in - W_A - 2 * W_B) // 2
    assert W_A + W_B + w_c == d_model and w_c % LANES == 0
    n_slab = w_c // LANES
    assert CONV_HALO >= CONV_K - 1 and tm % 256 == 0
    assert tm + CONV_HALO <= STRIDED_ROW_LIMIT
    x_spec = pl.BlockSpec((None, tm, d_model), lambda b, i: (b, i, 0))
    cs = functools.partial(_const_spec, layer=layer)
    return pl.pallas_call(
        _mixer_kernel,
        grid=(bsz, seq // tm),
        in_specs=[
            x_spec,
            cs((1, d_model)),
            cs((d_model, d_in)),
            cs((1, d_in)),
            cs((W_A, W_A)),
            cs((1, W_A)),
            cs((1, W_B)), cs((1, W_B)),
            cs((SG_HEADS * SG_BLOCK, SG_BLOCK)),
            cs((SG_BLOCK, W_B)),
            cs((CONV_K, w_c)), cs((1, w_c)),
            cs((1, w_c)), cs((1, w_c)),
            cs((d_model, d_model)),
            cs((1, d_model)),
        ],
        out_specs=x_spec,
        out_shape=jax.ShapeDtypeStruct(x.shape, x.dtype),
        scratch_shapes=[
            pltpu.VMEM((tm + POOL_HALO, W_A), F32),
            pltpu.VMEM((tm, d_model), BF16),
            pltpu.VMEM((tm, d_model), BF16),
            pltpu.VMEM((tm, d_in), F32),
        ] + [pltpu.VMEM((tm + CONV_HALO, LANES), F32)] * n_slab
          + [pltpu.VMEM((tm, LANES), F32)] * n_slab,
        compiler_params=pltpu.CompilerParams(
            dimension_semantics=("arbitrary", "arbitrary"),
            vmem_limit_bytes=VMEM_LIMIT),
        name=f"mixer_l{layer}",
    )(x, p["pre_mix_g"], p["w_in"], p["b_in"], p["pool_w"], p["pool_scale"],
      p["sg_ln_g"], p["sg_ln_b"], p["sg_w"], p["sg_b"],
      p["conv_w"], p["conv_b"], p["conv_ln_g"], p["conv_ln_b"], p["w_out"], p["post_mix_g"])


def _attn_kernel(x_ref, g1_ref, wqk_ref, vwo_ref, g2_ref, o_ref, hn_ref, s_ref, p_ref, acc_ref, rinv_ref):
    tm, d_model = x_ref.shape
    n_mem = d_model // X_HEADS
    hn_ref[...] = (x_ref[...] * g1_ref[...]).astype(BF16)
    s_ref[...] = _dot(hn_ref[...], wqk_ref[...])
    rows = 128
    for r0 in range(0, tm, rows):
        xr = x_ref[r0:r0 + rows, :]
        rinv = lax.rsqrt(jnp.mean(xr * xr, axis=-1, keepdims=True) + EPS)
        rinv_ref[r0:r0 + rows, :] = jnp.broadcast_to(rinv, (rows, LANES))
    for h in range(X_HEADS):
        hs = slice(h * n_mem, (h + 1) * n_mem)
        for r0 in range(0, tm, rows):
            rinv = rinv_ref[r0:r0 + rows, :]
            s = s_ref[r0:r0 + rows, hs] * jnp.concatenate([rinv] * (n_mem // LANES), axis=-1)
            e = jnp.exp(s - jnp.max(s, axis=-1, keepdims=True))
            p_ref[r0:r0 + rows, :] = (e / jnp.sum(e, axis=-1, keepdims=True)).astype(BF16)
        part = _dot(p_ref[...], vwo_ref[hs, :])
        if h == 0:
            acc_ref[...] = part
        else:
            acc_ref[...] += part
    o_ref[...] = x_ref[...] + _rms(acc_ref[...], g2_ref[...])


def _attention(x, layer, p, wqk, vwo):
    bsz, seq, d_model = x.shape
    tm = TM_ATT
    assert seq % tm == 0
    x_spec = pl.BlockSpec((None, tm, d_model), lambda b, i: (b, i, 0))
    kv_spec = pl.BlockSpec((None, None, d_model, d_model), lambda b, i: (layer, b, 0, 0))
    cs = functools.partial(_const_spec, layer=layer)
    return pl.pallas_call(
        _attn_kernel,
        grid=(bsz, seq // tm),
        in_specs=[x_spec, cs((1, d_model)), kv_spec, kv_spec, cs((1, d_model))],
        out_specs=x_spec,
        out_shape=jax.ShapeDtypeStruct(x.shape, x.dtype),
        scratch_shapes=[pltpu.VMEM((tm, d_model), BF16),
                        pltpu.VMEM((tm, d_model), F32),
                        pltpu.VMEM((tm, d_model // X_HEADS), BF16),
                        pltpu.VMEM((tm, d_model), F32),
                        pltpu.VMEM((tm, LANES), F32)],
        compiler_params=pltpu.CompilerParams(
            dimension_semantics=("arbitrary", "arbitrary"),
            vmem_limit_bytes=VMEM_LIMIT),
        name=f"attn_l{layer}",
    )(x, p["pre_x_g"], wqk, vwo, p["post_x_g"])


def _ffn_kernel(x_ref, g1_ref, w1_ref, w2_ref, g2_ref, o_ref, hn_ref, hc_ref):
    d_ff = w1_ref.shape[1]
    hn_ref[...] = _rms(x_ref[...], g1_ref[...]).astype(BF16)
    for c in range(d_ff // FF_CHUNK):
        cs = slice(c * FF_CHUNK, (c + 1) * FF_CHUNK)
        a = jnp.maximum(_dot(hn_ref[...], w1_ref[:, cs].astype(BF16)), 0.0)
        hc_ref[...] = (a * a).astype(BF16)
        part = _dot(hc_ref[...], w2_ref[cs, :].astype(BF16))
        if c == 0:
            o_ref[...] = part
        else:
            o_ref[...] += part
    o_ref[...] = x_ref[...] + _rms(o_ref[...], g2_ref[...])


def _ffn(x, layer, p):
    bsz, seq, d_model = x.shape
    d_ff = p["w_ff1"].shape[-1]
    tm = TM_FF
    assert seq % tm == 0 and d_ff % FF_CHUNK == 0
    x_spec = pl.BlockSpec((None, tm, d_model), lambda b, i: (b, i, 0))
    cs = functools.partial(_const_spec, layer=layer)
    return pl.pallas_call(
        _ffn_kernel,
        grid=(bsz, seq // tm),
        in_specs=[x_spec, cs((1, d_model)), cs((d_model, d_ff)), cs((d_ff, d_model)),
                  cs((1, d_model))],
        out_specs=x_spec,
        out_shape=jax.ShapeDtypeStruct(x.shape, x.dtype),
        scratch_shapes=[pltpu.VMEM((tm, d_model), BF16), pltpu.VMEM((tm, FF_CHUNK), BF16)],
        compiler_params=pltpu.CompilerParams(
            dimension_semantics=("arbitrary", "arbitrary"),
            vmem_limit_bytes=VMEM_LIMIT),
        name=f"ffn_l{layer}",
    )(x, p["pre_ff_g"], p["w_ff1"], p["w_ff2"], p["post_ff_g"])


def kernel(x, mem, pre_mix_g, w_in, b_in, pool_w, pool_scale, sg_ln_g, sg_ln_b, sg_w, sg_b, conv_w, conv_b, conv_ln_g, conv_ln_b, w_out, post_mix_g, pre_x_g, mem_g, wq, wk, wv, wo, post_x_g, pre_ff_g, w_ff1, w_ff2, post_ff_g):
    n_layers = w_in.shape[0]
    n_groups = pool_w.shape[1]

    def row(v):
        return v[:, None, :]

    eye = jnp.eye(n_groups, dtype=pool_w.dtype)
    pool_bd = jnp.einsum("lgcd,gh->lgchd", pool_w, eye).reshape(n_layers, W_A, W_A)
    sg_bias = jnp.repeat(jnp.swapaxes(sg_b, 1, 2), SG_HEAD_DIM, axis=2)
    p = {
        "pre_mix_g": row(pre_mix_g), "w_in": w_in, "b_in": row(b_in),
        "pool_w": pool_bd.astype(BF16), "pool_scale": row(pool_scale),
        "sg_ln_g": row(sg_ln_g), "sg_ln_b": row(sg_ln_b),
        "sg_w": sg_w.reshape(n_layers, SG_HEADS * SG_BLOCK, SG_BLOCK), "sg_b": sg_bias,
        "conv_w": conv_w, "conv_b": row(conv_b),
        "conv_ln_g": row(conv_ln_g), "conv_ln_b": row(conv_ln_b),
        "w_out": w_out, "post_mix_g": row(post_mix_g),
        "pre_x_g": row(pre_x_g), "post_x_g": row(post_x_g),
        "pre_ff_g": row(pre_ff_g), "w_ff1": w_ff1, "w_ff2": w_ff2,
        "post_ff_g": row(post_ff_g),
    }
    wqk, vwo = _fold_memory(mem, row(mem_g), wq, wk, wv, wo)
    for layer in range(n_layers):
        x = _mixer(x, layer, p)
        x = _attention(x, layer, p, wqk, vwo)
        x = _ffn(x, layer, p)
    return x
```

```python
import functools
import math

import jax
import jax.numpy as jnp
from jax import lax
from jax.experimental import pallas as pl
from jax.experimental.pallas import tpu as pltpu

EPS = 1e-6
BF16 = jnp.bfloat16
F32 = jnp.float32

POOL_WINDOWS = (2, 4, 8, 16)
POOL_CH = 64
W_A = len(POOL_WINDOWS) * POOL_CH
SG_BLOCK = 128
SG_HEADS = 4
SG_HEAD_DIM = 96
W_B = SG_HEADS * SG_HEAD_DIM
CONV_K = 31
X_HEADS = 4

TM_MIX = 1024
TM_ATT = 512
TM_FF = 1024
LANES = 128
CONV_ROW_STRIDE = 2
STRIDED_ROW_LIMIT = 2048
FF_CHUNK = 1024
POOL_HALO = 16
CONV_HALO = 32
VMEM_LIMIT = 56 * 1024 * 1024


def _dot(a, b):
    return jnp.dot(a, b, preferred_element_type=F32)


def _rms(xf, g):
    ms = jnp.mean(xf * xf, axis=-1, keepdims=True)
    return xf * lax.rsqrt(ms + EPS) * g


def _layer_norm(xf, g, b):
    mu = jnp.mean(xf, axis=-1, keepdims=True)
    xc = xf - mu
    var = jnp.mean(xc * xc, axis=-1, keepdims=True)
    return xc * lax.rsqrt(var + EPS) * g + b


def _vector(vec_ref, layout, name):
    start, size = layout[name]
    return vec_ref.at[:, pl.ds(start, size)]


def _kv_kernel(mem_ref, vec_ref, wq_ref, wk_ref, wv_ref, wo_ref, wqk_ref, vwo_ref, *, layout):
    g_ref = _vector(vec_ref, layout, "mem_g")
    d_model = mem_ref.shape[-1]
    dh = d_model // X_HEADS
    scale = 1.0 / math.sqrt(dh)
    m = _rms(mem_ref[...], g_ref[...]).astype(BF16)
    k = _dot(m, wk_ref[...].astype(BF16)).astype(BF16)
    v = _dot(m, wv_ref[...].astype(BF16)).astype(BF16)
    for h in range(X_HEADS):
        hs = slice(h * dh, (h + 1) * dh)
        s = lax.dot_general(wq_ref[:, hs].astype(BF16), k[:, hs], (((1,), (1,)), ((), ())),
                            preferred_element_type=F32)
        wqk_ref[:, hs] = (s * scale).astype(BF16)
        vwo_ref[hs, :] = _dot(v[:, hs], wo_ref[hs, :].astype(BF16)).astype(BF16)


def _fold_memory(mem, vectors, layout, wq, wk, wv, wo):
    n_layers, d_model, _ = wq.shape
    bsz, n_mem, _ = mem.shape
    assert n_mem * X_HEADS == d_model
    n_vec = vectors.shape[-1]
    w_spec = pl.BlockSpec((None, d_model, d_model), lambda l, b: (l, 0, 0))
    out_spec = pl.BlockSpec((None, None, d_model, d_model), lambda l, b: (l, b, 0, 0))
    out_shape = jax.ShapeDtypeStruct((n_layers, bsz, d_model, d_model), BF16)
    return pl.pallas_call(
        functools.partial(_kv_kernel, layout=layout),
        grid=(n_layers, bsz),
        in_specs=[
            pl.BlockSpec((None, n_mem, d_model), lambda l, b: (b, 0, 0)),
            pl.BlockSpec((None, 1, n_vec), lambda l, b: (l, 0, 0)),
            w_spec, w_spec, w_spec, w_spec,
        ],
        out_specs=[out_spec, out_spec],
        out_shape=[out_shape, out_shape],
        compiler_params=pltpu.CompilerParams(
            dimension_semantics=("arbitrary", "arbitrary"),
            vmem_limit_bytes=VMEM_LIMIT),
        name="fold_memory",
    )(mem, vectors, wq, wk, wv, wo)


def _mixer_kernel(x_ref, vec_ref, win_ref, poolw_ref, sgw_ref, sgbias_ref, convw_ref, wout_ref,
                  o_ref, abuf, ycat, hn_ref, z_ref, *slabs, layout):
    vec = functools.partial(_vector, vec_ref, layout)
    g1_ref, bin_ref, pscale_ref = vec("pre_mix_g"), vec("b_in"), vec("pool_scale")
    sglng_ref, sglnb_ref, convb_ref = vec("sg_ln_g"), vec("sg_ln_b"), vec("conv_b")
    clng_ref, clnb_ref, g2_ref = vec("conv_ln_g"), vec("conv_ln_b"), vec("post_mix_g")
    tm = x_ref.shape[0]
    n_slab = len(slabs) // 2
    hbufs, cbufs = slabs[:n_slab], slabs[n_slab:]
    w_c = n_slab * LANES
    off_b = W_A
    off_c = W_A + 2 * W_B
    i = pl.program_id(1)

    @pl.when(i == 0)
    def _():
        abuf[0:POOL_HALO, :] = jnp.zeros((POOL_HALO, W_A), F32)
        for hb in hbufs:
            hb[0:CONV_HALO, :] = jnp.zeros((CONV_HALO, LANES), F32)

    hn_ref[...] = _rms(x_ref[...], g1_ref[...]).astype(BF16)
    z_ref[...] = _dot(hn_ref[...], win_ref[...].astype(BF16)) + bin_ref[...]

    abuf[POOL_HALO:POOL_HALO + tm, :] = z_ref[:, 0:W_A]
    rows = 128
    lane = lax.broadcasted_iota(jnp.int32, (rows, W_A), 1)
    row = lax.broadcasted_iota(jnp.int32, (rows, W_A), 0)
    group = lane // POOL_CH
    window = jnp.full((rows, W_A), POOL_WINDOWS[-1], jnp.int32)
    for gi in range(len(POOL_WINDOWS) - 2, -1, -1):
        window = jnp.where(group == gi, POOL_WINDOWS[gi], window)
    for r0 in range(0, tm, rows):
        s = abuf[r0:r0 + POOL_HALO + rows, :]
        a0 = s[POOL_HALO:]
        win_sum = None
        w = 1
        for gi, target in enumerate(POOL_WINDOWS):
            while w < target:
                s = s + pltpu.roll(s, w, 0)
                w *= 2
            cur = s[POOL_HALO:]
            win_sum = cur if win_sum is None else jnp.where(group >= gi, cur, win_sum)
        pos1 = i * tm + r0 + row + 1
        cnt = jnp.minimum(pos1, window).astype(F32)
        p = (win_sum / cnt - a0).astype(BF16)
        ya = _dot(p, poolw_ref[...]) * pscale_ref[...]
        ycat[r0:r0 + rows, 0:W_A] = ya.astype(BF16)
    abuf[0:POOL_HALO, :] = abuf[tm:tm + POOL_HALO, :]

    wrow = lax.broadcasted_iota(jnp.int32, (SG_HEADS * SG_BLOCK, SG_BLOCK), 0)
    wcol = lax.broadcasted_iota(jnp.int32, (SG_HEADS * SG_BLOCK, SG_BLOCK), 1)
    causal = (wrow & (SG_BLOCK - 1)) >= wcol
    wst = jnp.where(causal, sgw_ref[...], 0.0).astype(BF16)
    hl = lax.broadcasted_iota(jnp.int32, (SG_BLOCK, W_B), 1)
    for blk in range(tm // SG_BLOCK):
        rs = slice(blk * SG_BLOCK, (blk + 1) * SG_BLOCK)
        v = _layer_norm(jax.nn.gelu(z_ref[rs, off_b + W_B:off_b + 2 * W_B]),
                        sglng_ref[...], sglnb_ref[...]).astype(BF16)
        res = _dot(wst, v)
        sv = res[(SG_HEADS - 1) * SG_BLOCK:SG_HEADS * SG_BLOCK]
        for h in range(SG_HEADS - 2, -1, -1):
            sv = jnp.where(hl < (h + 1) * SG_HEAD_DIM, res[h * SG_BLOCK:(h + 1) * SG_BLOCK], sv)
        yb = jax.nn.gelu(z_ref[rs, off_b:off_b + W_B]) * (sv + sgbias_ref[...])
        ycat[rs, W_A:W_A + W_B] = yb.astype(BF16)

    grows = 64
    for r0 in range(0, tm, grows):
        rs = slice(r0, r0 + grows)
        glu = z_ref[rs, off_c:off_c + w_c] * jax.nn.sigmoid(z_ref[rs, off_c + w_c:off_c + 2 * w_c])
        for c in range(n_slab):
            hbufs[c][CONV_HALO + r0:CONV_HALO + r0 + grows, :] = glu[:, c * LANES:(c + 1) * LANES]
    lead = CONV_HALO - (CONV_K - 1)
    crows = 256
    nrow = crows // CONV_ROW_STRIDE
    for c in range(n_slab):
        cs = slice(c * LANES, (c + 1) * LANES)
        for r0 in range(0, tm, crows):
            for ph in range(CONV_ROW_STRIDE):
                acc = None
                for k in range(CONV_K):
                    t = (hbufs[c][pl.ds(r0 + ph + lead + k, nrow, stride=CONV_ROW_STRIDE), :]
                         * convw_ref[k:k + 1, cs])
                    acc = t if acc is None else acc + t
                cbufs[c][pl.ds(r0 + ph, nrow, stride=CONV_ROW_STRIDE), :] = acc + convb_ref[:, cs]
    lrows = 128
    for r0 in range(0, tm, lrows):
        conv = jnp.concatenate([cbufs[c][r0:r0 + lrows, :] for c in range(n_slab)], axis=-1)
        hc = _layer_norm(conv, clng_ref[...], clnb_ref[...])
        ycat[r0:r0 + lrows, W_A + W_B:W_A + W_B + w_c] = jax.nn.silu(hc).astype(BF16)
    for c in range(n_slab):
        hbufs[c][0:CONV_HALO, :] = hbufs[c][tm:tm + CONV_HALO, :]

    y = _dot(ycat[...], wout_ref[...].astype(BF16))
    o_ref[...] = x_ref[...] + _rms(y, g2_ref[...])


def _const_spec(shape, layer):
    nd = len(shape)
    return pl.BlockSpec((None,) + tuple(shape), lambda b, i: (layer,) + (0,) * nd,
                        pipeline_mode=pl.Buffered(1))


def _mixer(x, layer, p):
    bsz, seq, d_model = x.shape
    tm = TM_MIX
    assert seq % tm == 0 and tm % SG_BLOCK == 0
    d_in = p["w_in"].shape[-1]
    w_c = (d_in - W_A - 2 * W_B) // 2
    assert W_A + W_B + w_c == d_model and w_c % LANES == 0
    n_slab = w_c // LANES
    assert CONV_HALO >= CONV_K - 1 and tm % 256 == 0
    assert tm + CONV_HALO <= STRIDED_ROW_LIMIT
    x_spec = pl.BlockSpec((None, tm, d_model), lambda b, i: (b, i, 0))
    cs = functools.partial(_const_spec, layer=layer)
    return pl.pallas_call(
        functools.partial(_mixer_kernel, layout=p["layout"]),
        grid=(bsz, seq // tm),
        in_specs=[
            x_spec,
            cs((1, p["vectors"].shape[-1])),
            cs((d_model, d_in)),
            cs((W_A, W_A)),
            cs((SG_HEADS * SG_BLOCK, SG_BLOCK)),
            cs((SG_BLOCK, W_B)),
            cs((CONV_K, w_c)),
            cs((d_model, d_model)),
        ],
        out_specs=x_spec,
        out_shape=jax.ShapeDtypeStruct(x.shape, x.dtype),
        scratch_shapes=[
            pltpu.VMEM((tm + POOL_HALO, W_A), F32),
            pltpu.VMEM((tm, d_model), BF16),
            pltpu.VMEM((tm, d_model), BF16),
            pltpu.VMEM((tm, d_in), F32),
        ] + [pltpu.VMEM((tm + CONV_HALO, LANES), F32)] * n_slab
          + [pltpu.VMEM((tm, LANES), F32)] * n_slab,
        compiler_params=pltpu.CompilerParams(
            dimension_semantics=("arbitrary", "arbitrary"),
            vmem_limit_bytes=VMEM_LIMIT),
        name=f"mixer_l{layer}",
    )(x, p["vectors"], p["w_in"], p["pool_w"], p["sg_w"], p["sg_b"], p["conv_w"], p["w_out"])


def _attn_kernel(x_ref, vec_ref, wqk_ref, vwo_ref, o_ref, hn_ref, s_ref, p_ref, acc_ref, rinv_ref,
                 *, layout):
    g1_ref, g2_ref = _vector(vec_ref, layout, "pre_x_g"), _vector(vec_ref, layout, "post_x_g")
    tm, d_model = x_ref.shape
    n_mem = d_model // X_HEADS
    hn_ref[...] = (x_ref[...] * g1_ref[...]).astype(BF16)
    s_ref[...] = _dot(hn_ref[...], wqk_ref[...])
    rows = 128
    for r0 in range(0, tm, rows):
        xr = x_ref[r0:r0 + rows, :]
        rinv = lax.rsqrt(jnp.mean(xr * xr, axis=-1, keepdims=True) + EPS)
        rinv_ref[r0:r0 + rows, :] = jnp.broadcast_to(rinv, (rows, LANES))
    for h in range(X_HEADS):
        hs = slice(h * n_mem, (h + 1) * n_mem)
        for r0 in range(0, tm, rows):
            rinv = rinv_ref[r0:r0 + rows, :]
            s = s_ref[r0:r0 + rows, hs] * jnp.concatenate([rinv] * (n_mem // LANES), axis=-1)
            e = jnp.exp(s - jnp.max(s, axis=-1, keepdims=True))
            p_ref[r0:r0 + rows, :] = (e / jnp.sum(e, axis=-1, keepdims=True)).astype(BF16)
        part = _dot(p_ref[...], vwo_ref[hs, :])
        if h == 0:
            acc_ref[...] = part
        else:
            acc_ref[...] += part
    o_ref[...] = x_ref[...] + _rms(acc_ref[...], g2_ref[...])


def _attention(x, layer, p, wqk, vwo):
    bsz, seq, d_model = x.shape
    tm = TM_ATT
    assert seq % tm == 0
    x_spec = pl.BlockSpec((None, tm, d_model), lambda b, i: (b, i, 0))
    kv_spec = pl.BlockSpec((None, None, d_model, d_model), lambda b, i: (layer, b, 0, 0))
    cs = functools.partial(_const_spec, layer=layer)
    return pl.pallas_call(
        functools.partial(_attn_kernel, layout=p["layout"]),
        grid=(bsz, seq // tm),
        in_specs=[x_spec, cs((1, p["vectors"].shape[-1])), kv_spec, kv_spec],
        out_specs=x_spec,
        out_shape=jax.ShapeDtypeStruct(x.shape, x.dtype),
        scratch_shapes=[pltpu.VMEM((tm, d_model), BF16),
                        pltpu.VMEM((tm, d_model), F32),
                        pltpu.VMEM((tm, d_model // X_HEADS), BF16),
                        pltpu.VMEM((tm, d_model), F32),
                        pltpu.VMEM((tm, LANES), F32)],
        compiler_params=pltpu.CompilerParams(
            dimension_semantics=("arbitrary", "arbitrary"),
            vmem_limit_bytes=VMEM_LIMIT),
        name=f"attn_l{layer}",
    )(x, p["vectors"], wqk, vwo)


def _ffn_kernel(x_ref, vec_ref, w1_ref, w2_ref, o_ref, hn_ref, hc_ref, *, layout):
    g1_ref, g2_ref = _vector(vec_ref, layout, "pre_ff_g"), _vector(vec_ref, layout, "post_ff_g")
    d_ff = w1_ref.shape[1]
    hn_ref[...] = _rms(x_ref[...], g1_ref[...]).astype(BF16)
    for c in range(d_ff // FF_CHUNK):
        cs = slice(c * FF_CHUNK, (c + 1) * FF_CHUNK)
        a = jnp.maximum(_dot(hn_ref[...], w1_ref[:, cs].astype(BF16)), 0.0)
        hc_ref[...] = (a * a).astype(BF16)
        part = _dot(hc_ref[...], w2_ref[cs, :].astype(BF16))
        if c == 0:
            o_ref[...] = part
        else:
            o_ref[...] += part
    o_ref[...] = x_ref[...] + _rms(o_ref[...], g2_ref[...])


def _ffn(x, layer, p):
    bsz, seq, d_model = x.shape
    d_ff = p["w_ff1"].shape[-1]
    tm = TM_FF
    assert seq % tm == 0 and d_ff % FF_CHUNK == 0
    x_spec = pl.BlockSpec((None, tm, d_model), lambda b, i: (b, i, 0))
    cs = functools.partial(_const_spec, layer=layer)
    return pl.pallas_call(
        functools.partial(_ffn_kernel, layout=p["layout"]),
        grid=(bsz, seq // tm),
        in_specs=[x_spec, cs((1, p["vectors"].shape[-1])), cs((d_model, d_ff)), cs((d_ff, d_model))],
        out_specs=x_spec,
        out_shape=jax.ShapeDtypeStruct(x.shape, x.dtype),
        scratch_shapes=[pltpu.VMEM((tm, d_model), BF16), pltpu.VMEM((tm, FF_CHUNK), BF16)],
        compiler_params=pltpu.CompilerParams(
            dimension_semantics=("arbitrary", "arbitrary"),
            vmem_limit_bytes=VMEM_LIMIT),
        name=f"ffn_l{layer}",
    )(x, p["vectors"], p["w_ff1"], p["w_ff2"])


def kernel(x, mem, pre_mix_g, w_in, b_in, pool_w, pool_scale, sg_ln_g, sg_ln_b, sg_w, sg_b, conv_w, conv_b, conv_ln_g, conv_ln_b, w_out, post_mix_g, pre_x_g, mem_g, wq, wk, wv, wo, post_x_g, pre_ff_g, w_ff1, w_ff2, post_ff_g):
    n_layers = w_in.shape[0]
    n_groups = pool_w.shape[1]

    vectors = {
        "pre_mix_g": pre_mix_g, "b_in": b_in, "pool_scale": pool_scale,
        "sg_ln_g": sg_ln_g, "sg_ln_b": sg_ln_b, "conv_b": conv_b,
        "conv_ln_g": conv_ln_g, "conv_ln_b": conv_ln_b, "post_mix_g": post_mix_g,
        "pre_x_g": pre_x_g, "post_x_g": post_x_g, "pre_ff_g": pre_ff_g,
        "post_ff_g": post_ff_g, "mem_g": mem_g,
    }
    layout, offset = {}, 0
    for name, v in vectors.items():
        assert v.shape[-1] % LANES == 0
        layout[name] = (offset, v.shape[-1])
        offset += v.shape[-1]
    packed = jnp.concatenate(list(vectors.values()), axis=-1)[:, None, :]
    eye = jnp.eye(n_groups, dtype=pool_w.dtype)
    pool_bd = jnp.einsum("lgcd,gh->lgchd", pool_w, eye).reshape(n_layers, W_A, W_A)
    sg_bias = jnp.repeat(jnp.swapaxes(sg_b, 1, 2), SG_HEAD_DIM, axis=2)
    p = {
        "vectors": packed, "layout": layout, "w_in": w_in,
        "pool_w": pool_bd.astype(BF16),
        "sg_w": sg_w.reshape(n_layers, SG_HEADS * SG_BLOCK, SG_BLOCK), "sg_b": sg_bias,
        "conv_w": conv_w, "w_out": w_out, "w_ff1": w_ff1, "w_ff2": w_ff2,
    }
    wqk, vwo = _fold_memory(mem, packed, layout, wq, wk, wv, wo)
    for layer in range(n_layers):
        x = _mixer(x, layer, p)
        x = _attention(x, layer, p, wqk, vwo)
        x = _ffn(x, layer, p)
    return x
```

```python
import functools
import math

import jax
import jax.numpy as jnp
from jax import lax
from jax.experimental import pallas as pl
from jax.experimental.pallas import tpu as pltpu

EPS = 1e-6
BF16 = jnp.bfloat16
F32 = jnp.float32

POOL_WINDOWS = (2, 4, 8, 16)
POOL_CH = 64
W_A = len(POOL_WINDOWS) * POOL_CH
SG_BLOCK = 128
SG_HEADS = 4
SG_HEAD_DIM = 96
W_B = SG_HEADS * SG_HEAD_DIM
CONV_K = 31
X_HEADS = 4

TM_MIX = 1024
TM_ATT = 512
TM_FF = 1024
LANES = 128
CONV_ROW_STRIDE = 2
STRIDED_ROW_LIMIT = 2048
FF_CHUNK = 1024
POOL_HALO = 16
CONV_HALO = 32
VMEM_LIMIT = 56 * 1024 * 1024


def _dot(a, b):
    return jnp.dot(a, b, preferred_element_type=F32)


def _rms(xf, g):
    ms = jnp.mean(xf * xf, axis=-1, keepdims=True)
    return xf * lax.rsqrt(ms + EPS) * g


def _layer_norm(xf, g, b):
    mu = jnp.mean(xf, axis=-1, keepdims=True)
    xc = xf - mu
    var = jnp.mean(xc * xc, axis=-1, keepdims=True)
    return xc * lax.rsqrt(var + EPS) * g + b


def _vector(vec_ref, layout, name):
    start, size = layout[name]
    return vec_ref.at[:, pl.ds(start, size)]


def _kv_kernel(mem_ref, vec_ref, wq_ref, wk_ref, wv_ref, wo_ref, wqk_ref, vwo_ref, *, layout):
    g_ref = _vector(vec_ref, layout, "mem_g")
    d_model = mem_ref.shape[-1]
    dh = d_model // X_HEADS
    scale = 1.0 / math.sqrt(dh)
    m = _rms(mem_ref[...], g_ref[...]).astype(BF16)
    k = _dot(m, wk_ref[...].astype(BF16)).astype(BF16)
    v = _dot(m, wv_ref[...].astype(BF16)).astype(BF16)
    for h in range(X_HEADS):
        hs = slice(h * dh, (h + 1) * dh)
        s = lax.dot_general(wq_ref[:, hs].astype(BF16), k[:, hs], (((1,), (1,)), ((), ())),
                            preferred_element_type=F32)
        wqk_ref[:, hs] = (s * scale).astype(BF16)
        vwo_ref[hs, :] = _dot(v[:, hs], wo_ref[hs, :].astype(BF16)).astype(BF16)


def _fold_memory(mem, vectors, layout, wq, wk, wv, wo):
    n_layers, d_model, _ = wq.shape
    bsz, n_mem, _ = mem.shape
    assert n_mem * X_HEADS == d_model
    n_vec = vectors.shape[-1]
    w_spec = pl.BlockSpec((None, d_model, d_model), lambda l, b: (l, 0, 0))
    out_spec = pl.BlockSpec((None, None, d_model, d_model), lambda l, b: (l, b, 0, 0))
    out_shape = jax.ShapeDtypeStruct((n_layers, bsz, d_model, d_model), BF16)
    return pl.pallas_call(
        functools.partial(_kv_kernel, layout=layout),
        grid=(n_layers, bsz),
        in_specs=[
            pl.BlockSpec((None, n_mem, d_model), lambda l, b: (b, 0, 0)),
            pl.BlockSpec((None, 1, n_vec), lambda l, b: (l, 0, 0)),
            w_spec, w_spec, w_spec, w_spec,
        ],
        out_specs=[out_spec, out_spec],
        out_shape=[out_shape, out_shape],
        compiler_params=pltpu.CompilerParams(
            dimension_semantics=("arbitrary", "arbitrary"),
            vmem_limit_bytes=VMEM_LIMIT),
        name="fold_memory",
    )(mem, vectors, wq, wk, wv, wo)


def _mixer_kernel(x_ref, vec_ref, win_ref, poolw_ref, sgw_ref, sgbias_ref, convw_ref, wout_ref,
                  o_ref, abuf, ycat, hn_ref, z_ref, *slabs, layout):
    vec = functools.partial(_vector, vec_ref, layout)
    g1_ref, bin_ref, pscale_ref = vec("pre_mix_g"), vec("b_in"), vec("pool_scale")
    sglng_ref, sglnb_ref, convb_ref = vec("sg_ln_g"), vec("sg_ln_b"), vec("conv_b")
    clng_ref, clnb_ref, g2_ref = vec("conv_ln_g"), vec("conv_ln_b"), vec("post_mix_g")
    tm = x_ref.shape[0]
    n_slab = len(slabs) // 2
    hbufs, cbufs = slabs[:n_slab], slabs[n_slab:]
    w_c = n_slab * LANES
    off_b = W_A
    off_c = W_A + 2 * W_B
    i = pl.program_id(1)

    @pl.when(i == 0)
    def _():
        abuf[0:POOL_HALO, :] = jnp.zeros((POOL_HALO, W_A), F32)
        for hb in hbufs:
            hb[0:CONV_HALO, :] = jnp.zeros((CONV_HALO, LANES), F32)

    hn_ref[...] = _rms(x_ref[...], g1_ref[...]).astype(BF16)
    z_ref[...] = _dot(hn_ref[...], win_ref[...].astype(BF16)) + bin_ref[...]

    abuf[POOL_HALO:POOL_HALO + tm, :] = z_ref[:, 0:W_A]
    rows = 128
    lane = lax.broadcasted_iota(jnp.int32, (rows, W_A), 1)
    row = lax.broadcasted_iota(jnp.int32, (rows, W_A), 0)
    group = lane // POOL_CH
    window = jnp.full((rows, W_A), POOL_WINDOWS[-1], jnp.int32)
    for gi in range(len(POOL_WINDOWS) - 2, -1, -1):
        window = jnp.where(group == gi, POOL_WINDOWS[gi], window)
    for r0 in range(0, tm, rows):
        s = abuf[r0:r0 + POOL_HALO + rows, :]
        a0 = s[POOL_HALO:]
        win_sum = None
        w = 1
        for gi, target in enumerate(POOL_WINDOWS):
            while w < target:
                s = s + pltpu.roll(s, w, 0)
                w *= 2
            cur = s[POOL_HALO:]
            win_sum = cur if win_sum is None else jnp.where(group >= gi, cur, win_sum)
        pos1 = i * tm + r0 + row + 1
        cnt = jnp.minimum(pos1, window).astype(F32)
        p = (win_sum / cnt - a0).astype(BF16)
        ya = _dot(p, poolw_ref[...]) * pscale_ref[...]
        ycat[r0:r0 + rows, 0:W_A] = ya.astype(BF16)
    abuf[0:POOL_HALO, :] = abuf[tm:tm + POOL_HALO, :]

    wrow = lax.broadcasted_iota(jnp.int32, (SG_HEADS * SG_BLOCK, SG_BLOCK), 0)
    wcol = lax.broadcasted_iota(jnp.int32, (SG_HEADS * SG_BLOCK, SG_BLOCK), 1)
    causal = (wrow & (SG_BLOCK - 1)) >= wcol
    wst = jnp.where(causal, sgw_ref[...], 0.0).astype(BF16)
    hl = lax.broadcasted_iota(jnp.int32, (SG_BLOCK, W_B), 1)
    for blk in range(tm // SG_BLOCK):
        rs = slice(blk * SG_BLOCK, (blk + 1) * SG_BLOCK)
        v = _layer_norm(jax.nn.gelu(z_ref[rs, off_b + W_B:off_b + 2 * W_B]),
                        sglng_ref[...], sglnb_ref[...]).astype(BF16)
        res = _dot(wst, v)
        sv = res[(SG_HEADS - 1) * SG_BLOCK:SG_HEADS * SG_BLOCK]
        for h in range(SG_HEADS - 2, -1, -1):
            sv = jnp.where(hl < (h + 1) * SG_HEAD_DIM, res[h * SG_BLOCK:(h + 1) * SG_BLOCK], sv)
        yb = jax.nn.gelu(z_ref[rs, off_b:off_b + W_B]) * (sv + sgbias_ref[...])
        ycat[rs, W_A:W_A + W_B] = yb.astype(BF16)

    grows = 64
    for r0 in range(0, tm, grows):
        rs = slice(r0, r0 + grows)
        glu = z_ref[rs, off_c:off_c + w_c] * jax.nn.sigmoid(z_ref[rs, off_c + w_c:off_c + 2 * w_c])
        for c in range(n_slab):
            hbufs[c][CONV_HALO + r0:CONV_HALO + r0 + grows, :] = glu[:, c * LANES:(c + 1) * LANES]
    lead = CONV_HALO - (CONV_K - 1)
    crows = 256
    nrow = crows // CONV_ROW_STRIDE
    for c in range(n_slab):
        cs = slice(c * LANES, (c + 1) * LANES)
        for r0 in range(0, tm, crows):
            for ph in range(CONV_ROW_STRIDE):
                acc = None
                for k in range(CONV_K):
                    t = (hbufs[c][pl.ds(r0 + ph + lead + k, nrow, stride=CONV_ROW_STRIDE), :]
                         * convw_ref[k:k + 1, cs])
                    acc = t if acc is None else acc + t
                cbufs[c][pl.ds(r0 + ph, nrow, stride=CONV_ROW_STRIDE), :] = acc + convb_ref[:, cs]
    lrows = 128
    for r0 in range(0, tm, lrows):
        conv = jnp.concatenate([cbufs[c][r0:r0 + lrows, :] for c in range(n_slab)], axis=-1)
        hc = _layer_norm(conv, clng_ref[...], clnb_ref[...])
        ycat[r0:r0 + lrows, W_A + W_B:W_A + W_B + w_c] = jax.nn.silu(hc).astype(BF16)
    for c in range(n_slab):
        hbufs[c][0:CONV_HALO, :] = hbufs[c][tm:tm + CONV_HALO, :]

    y = _dot(ycat[...], wout_ref[...].astype(BF16))
    o_ref[...] = x_ref[...] + _rms(y, g2_ref[...])


def _const_spec(shape, layer):
    nd = len(shape)
    return pl.BlockSpec((None,) + tuple(shape), lambda b, i: (layer,) + (0,) * nd,
                        pipeline_mode=pl.Buffered(1))


def _mixer(x, layer, p):
    bsz, seq, d_model = x.shape
    tm = TM_MIX
    assert seq % tm == 0 and tm % SG_BLOCK == 0
    d_in = p["w_in"].shape[-1]
    w_c = (d_in - W_A - 2 * W_B) // 2
    assert W_A + W_B + w_c == d_model and w_c % LANES == 0
    n_slab = w_c // LANES
    assert CONV_HALO >= CONV_K - 1 and tm % 256 == 0
    assert tm + CONV_HALO <= STRIDED_ROW_LIMIT
    x_spec = pl.BlockSpec((None, tm, d_model), lambda b, i: (b, i, 0))
    cs = functools.partial(_const_spec, layer=layer)
    return pl.pallas_call(
        functools.partial(_mixer_kernel, layout=p["layout"]),
        grid=(bsz, seq // tm),
        in_specs=[
            x_spec,
            cs((1, p["vectors"].shape[-1])),
            cs((d_model, d_in)),
            cs((W_A, W_A)),
            cs((SG_HEADS * SG_BLOCK, SG_BLOCK)),
            cs((SG_BLOCK, W_B)),
            cs((CONV_K, w_c)),
            cs((d_model, d_model)),
        ],
        out_specs=x_spec,
        out_shape=jax.ShapeDtypeStruct(x.shape, x.dtype),
        scratch_shapes=[
            pltpu.VMEM((tm + POOL_HALO, W_A), F32),
            pltpu.VMEM((tm, d_model), BF16),
            pltpu.VMEM((tm, d_model), BF16),
            pltpu.VMEM((tm, d_in), F32),
        ] + [pltpu.VMEM((tm + CONV_HALO, LANES), F32)] * n_slab
          + [pltpu.VMEM((tm, LANES), F32)] * n_slab,
        compiler_params=pltpu.CompilerParams(
            dimension_semantics=("arbitrary", "arbitrary"),
            vmem_limit_bytes=VMEM_LIMIT),
        name=f"mixer_l{layer}",
    )(x, p["vectors"], p["w_in"], p["pool_w"], p["sg_w"], p["sg_b"], p["conv_w"], p["w_out"])


def _attn_kernel(x_ref, vec_ref, wqk_ref, vwo_ref, o_ref, hn_ref, s_ref, p_ref, acc_ref, rinv_ref,
                 *, layout):
    g1_ref, g2_ref = _vector(vec_ref, layout, "pre_x_g"), _vector(vec_ref, layout, "post_x_g")
    tm, d_model = x_ref.shape
    n_mem = d_model // X_HEADS
    hn_ref[...] = (x_ref[...] * g1_ref[...]).astype(BF16)
    s_ref[...] = _dot(hn_ref[...], wqk_ref[...])
    rows = 128
    for r0 in range(0, tm, rows):
        xr = x_ref[r0:r0 + rows, :]
        rinv = lax.rsqrt(jnp.mean(xr * xr, axis=-1, keepdims=True) + EPS)
        rinv_ref[r0:r0 + rows, :] = jnp.broadcast_to(rinv, (rows, LANES))
    for h in range(X_HEADS):
        hs = slice(h * n_mem, (h + 1) * n_mem)
        for r0 in range(0, tm, rows):
            rinv = rinv_ref[r0:r0 + rows, :]
            s = s_ref[r0:r0 + rows, hs] * jnp.concatenate([rinv] * (n_mem // LANES), axis=-1)
            e = jnp.exp(s - jnp.max(s, axis=-1, keepdims=True))
            p_ref[r0:r0 + rows, :] = (e / jnp.sum(e, axis=-1, keepdims=True)).astype(BF16)
        part = _dot(p_ref[...], vwo_ref[hs, :])
        if h == 0:
            acc_ref[...] = part
        else:
            acc_ref[...] += part
    o_ref[...] = x_ref[...] + _rms(acc_ref[...], g2_ref[...])


def _attention(x, layer, p, wqk, vwo):
    bsz, seq, d_model = x.shape
    tm = TM_ATT
    assert seq % tm == 0
    x_spec = pl.BlockSpec((None, tm, d_model), lambda b, i: (b, i, 0))
    kv_spec = pl.BlockSpec((None, None, d_model, d_model), lambda b, i: (layer, b, 0, 0))
    cs = functools.partial(_const_spec, layer=layer)
    return pl.pallas_call(
        functools.partial(_attn_kernel, layout=p["layout"]),
        grid=(bsz, seq // tm),
        in_specs=[x_spec, cs((1, p["vectors"].shape[-1])), kv_spec, kv_spec],
        out_specs=x_spec,
        out_shape=jax.ShapeDtypeStruct(x.shape, x.dtype),
        scratch_shapes=[pltpu.VMEM((tm, d_model), BF16),
                        pltpu.VMEM((tm, d_model), F32),
                        pltpu.VMEM((tm, d_model // X_HEADS), BF16),
                        pltpu.VMEM((tm, d_model), F32),
                        pltpu.VMEM((tm, LANES), F32)],
        compiler_params=pltpu.CompilerParams(
            dimension_semantics=("arbitrary", "arbitrary"),
            vmem_limit_bytes=VMEM_LIMIT),
        name=f"attn_l{layer}",
    )(x, p["vectors"], wqk, vwo)


def _ffn_kernel(x_ref, vec_ref, w1_hbm, w2_hbm, o_ref, hn_ref, hc_ref, w1_ref, w2_ref, sem, *, layout, layer):
    g1_ref, g2_ref = _vector(vec_ref, layout, "pre_ff_g"), _vector(vec_ref, layout, "post_ff_g")
    d_ff = w1_ref.shape[1]
    n_chunks = d_ff // FF_CHUNK
    first = jnp.logical_and(pl.program_id(0) == 0, pl.program_id(1) == 0)

    def weight_copies(c):
        cs = pl.ds(c * FF_CHUNK, FF_CHUNK)
        return (pltpu.make_async_copy(w1_hbm.at[layer, :, cs], w1_ref.at[:, cs], sem.at[0, c]),
                pltpu.make_async_copy(w2_hbm.at[layer, cs, :], w2_ref.at[cs, :], sem.at[1, c]))

    @pl.when(first)
    def _():
        for c in range(n_chunks):
            for cp in weight_copies(c):
                cp.start()

    hn_ref[...] = _rms(x_ref[...], g1_ref[...]).astype(BF16)

    @pl.when(first)
    def _():
        for cp in weight_copies(0):
            cp.wait()

    for c in range(n_chunks):
        if c == 1:
            @pl.when(first)
            def _():
                for cc in range(1, n_chunks):
                    for cp in weight_copies(cc):
                        cp.wait()
        cs = slice(c * FF_CHUNK, (c + 1) * FF_CHUNK)
        a = jnp.maximum(_dot(hn_ref[...], w1_ref[:, cs].astype(BF16)), 0.0)
        hc_ref[...] = (a * a).astype(BF16)
        part = _dot(hc_ref[...], w2_ref[cs, :].astype(BF16))
        if c == 0:
            o_ref[...] = part
        else:
            o_ref[...] += part
    o_ref[...] = x_ref[...] + _rms(o_ref[...], g2_ref[...])


def _ffn(x, layer, p):
    bsz, seq, d_model = x.shape
    d_ff = p["w_ff1"].shape[-1]
    tm = TM_FF
    assert seq % tm == 0 and d_ff % FF_CHUNK == 0
    x_spec = pl.BlockSpec((None, tm, d_model), lambda b, i: (b, i, 0))
    cs = functools.partial(_const_spec, layer=layer)
    return pl.pallas_call(
        functools.partial(_ffn_kernel, layout=p["layout"], layer=layer),
        grid=(bsz, seq // tm),
        in_specs=[x_spec, cs((1, p["vectors"].shape[-1])),
                  pl.BlockSpec(memory_space=pl.ANY), pl.BlockSpec(memory_space=pl.ANY)],
        out_specs=x_spec,
        out_shape=jax.ShapeDtypeStruct(x.shape, x.dtype),
        scratch_shapes=[pltpu.VMEM((tm, d_model), BF16), pltpu.VMEM((tm, FF_CHUNK), BF16),
                        pltpu.VMEM((d_model, d_ff), F32), pltpu.VMEM((d_ff, d_model), F32),
                        pltpu.SemaphoreType.DMA((2, d_ff // FF_CHUNK))],
        compiler_params=pltpu.CompilerParams(
            dimension_semantics=("arbitrary", "arbitrary"),
            vmem_limit_bytes=VMEM_LIMIT),
        name=f"ffn_l{layer}",
    )(x, p["vectors"], p["w_ff1"], p["w_ff2"])


def kernel(x, mem, pre_mix_g, w_in, b_in, pool_w, pool_scale, sg_ln_g, sg_ln_b, sg_w, sg_b, conv_w, conv_b, conv_ln_g, conv_ln_b, w_out, post_mix_g, pre_x_g, mem_g, wq, wk, wv, wo, post_x_g, pre_ff_g, w_ff1, w_ff2, post_ff_g):
    n_layers = w_in.shape[0]
    n_groups = pool_w.shape[1]

    vectors = {
        "pre_mix_g": pre_mix_g, "b_in": b_in, "pool_scale": pool_scale,
        "sg_ln_g": sg_ln_g, "sg_ln_b": sg_ln_b, "conv_b": conv_b,
        "conv_ln_g": conv_ln_g, "conv_ln_b": conv_ln_b, "post_mix_g": post_mix_g,
        "pre_x_g": pre_x_g, "post_x_g": post_x_g, "pre_ff_g": pre_ff_g,
        "post_ff_g": post_ff_g, "mem_g": mem_g,
    }
    layout, offset = {}, 0
    for name, v in vectors.items():
        assert v.shape[-1] % LANES == 0
        layout[name] = (offset, v.shape[-1])
        offset += v.shape[-1]
    packed = jnp.concatenate(list(vectors.values()), axis=-1)[:, None, :]
    eye = jnp.eye(n_groups, dtype=pool_w.dtype)
    pool_bd = jnp.einsum("lgcd,gh->lgchd", pool_w, eye).reshape(n_layers, W_A, W_A)
    sg_bias = jnp.repeat(jnp.swapaxes(sg_b, 1, 2), SG_HEAD_DIM, axis=2)
    p = {
        "vectors": packed, "layout": layout, "w_in": w_in,
        "pool_w": pool_bd.astype(BF16),
        "sg_w": sg_w.reshape(n_layers, SG_HEADS * SG_BLOCK, SG_BLOCK), "sg_b": sg_bias,
        "conv_w": conv_w, "w_out": w_out, "w_ff1": w_ff1, "w_ff2": w_ff2,
    }
    wqk, vwo = _fold_memory(mem, packed, layout, wq, wk, wv, wo)
    for layer in range(n_layers):
        x = _mixer(x, layer, p)
        x = _attention(x, layer, p, wqk, vwo)
        x = _ffn(x, layer, p)
    return x
```

```python
import functools
import math

import jax
import jax.numpy as jnp
from jax import lax
from jax.experimental import pallas as pl
from jax.experimental.pallas import tpu as pltpu

EPS = 1e-6
BF16 = jnp.bfloat16
F32 = jnp.float32

POOL_WINDOWS = (2, 4, 8, 16)
POOL_CH = 64
W_A = len(POOL_WINDOWS) * POOL_CH
SG_BLOCK = 128
SG_HEADS = 4
SG_HEAD_DIM = 96
W_B = SG_HEADS * SG_HEAD_DIM
CONV_K = 31
X_HEADS = 4

TM_MIX = 1024
TM_ATT = 512
TM_FF = 1024
LANES = 128
CONV_ROW_STRIDE = 2
STRIDED_ROW_LIMIT = 2048
FF_CHUNK = 1024
POOL_HALO = 16
CONV_HALO = 32
VMEM_LIMIT = 56 * 1024 * 1024


def _dot(a, b):
    return jnp.dot(a, b, preferred_element_type=F32)


def _rms(xf, g):
    ms = jnp.mean(xf * xf, axis=-1, keepdims=True)
    return xf * lax.rsqrt(ms + EPS) * g


def _layer_norm(xf, g, b):
    mu = jnp.mean(xf, axis=-1, keepdims=True)
    xc = xf - mu
    var = jnp.mean(xc * xc, axis=-1, keepdims=True)
    return xc * lax.rsqrt(var + EPS) * g + b


def _vector(vec_ref, layout, name):
    start, size = layout[name]
    return vec_ref.at[:, pl.ds(start, size)]


def _kv_kernel(mem_ref, vec_ref, wq_ref, wk_ref, wv_ref, wo_ref, wqk_ref, vwo_ref, *, layout):
    g_ref = _vector(vec_ref, layout, "mem_g")
    d_model = mem_ref.shape[-1]
    dh = d_model // X_HEADS
    scale = 1.0 / math.sqrt(dh)
    m = _rms(mem_ref[...], g_ref[...]).astype(BF16)
    k = _dot(m, wk_ref[...].astype(BF16)).astype(BF16)
    v = _dot(m, wv_ref[...].astype(BF16)).astype(BF16)
    for h in range(X_HEADS):
        hs = slice(h * dh, (h + 1) * dh)
        s = lax.dot_general(wq_ref[:, hs].astype(BF16), k[:, hs], (((1,), (1,)), ((), ())),
                            preferred_element_type=F32)
        wqk_ref[:, hs] = (s * scale).astype(BF16)
        vwo_ref[hs, :] = _dot(v[:, hs], wo_ref[hs, :].astype(BF16)).astype(BF16)


def _fold_memory(mem, vectors, layout, wq, wk, wv, wo):
    n_layers, d_model, _ = wq.shape
    bsz, n_mem, _ = mem.shape
    assert n_mem * X_HEADS == d_model
    n_vec = vectors.shape[-1]
    w_spec = pl.BlockSpec((None, d_model, d_model), lambda l, b: (l, 0, 0))
    out_spec = pl.BlockSpec((None, None, d_model, d_model), lambda l, b: (l, b, 0, 0))
    out_shape = jax.ShapeDtypeStruct((n_layers, bsz, d_model, d_model), BF16)
    return pl.pallas_call(
        functools.partial(_kv_kernel, layout=layout),
        grid=(n_layers, bsz),
        in_specs=[
            pl.BlockSpec((None, n_mem, d_model), lambda l, b: (b, 0, 0)),
            pl.BlockSpec((None, 1, n_vec), lambda l, b: (l, 0, 0)),
            w_spec, w_spec, w_spec, w_spec,
        ],
        out_specs=[out_spec, out_spec],
        out_shape=[out_shape, out_shape],
        compiler_params=pltpu.CompilerParams(
            dimension_semantics=("arbitrary", "arbitrary"),
            vmem_limit_bytes=VMEM_LIMIT),
        name="fold_memory",
    )(mem, vectors, wq, wk, wv, wo)


def _mixer_kernel(x_ref, vec_ref, win_ref, poolw_ref, sgw_ref, sgbias_ref, convw_ref, wout_ref,
                  o_ref, abuf, ycat, hn_ref, z_ref, *slabs, layout):
    vec = functools.partial(_vector, vec_ref, layout)
    g1_ref, bin_ref, pscale_ref = vec("pre_mix_g"), vec("b_in"), vec("pool_scale")
    sglng_ref, sglnb_ref, convb_ref = vec("sg_ln_g"), vec("sg_ln_b"), vec("conv_b")
    clng_ref, clnb_ref, g2_ref = vec("conv_ln_g"), vec("conv_ln_b"), vec("post_mix_g")
    tm = x_ref.shape[0]
    n_slab = len(slabs) // 2
    hbufs, cbufs = slabs[:n_slab], slabs[n_slab:]
    w_c = n_slab * LANES
    off_b = W_A
    off_c = W_A + 2 * W_B
    i = pl.program_id(1)

    keep = i != 0
    abuf[0:POOL_HALO, :] = jnp.where(keep, abuf[0:POOL_HALO, :], 0.0)
    for hb in hbufs:
        hb[0:CONV_HALO, :] = jnp.where(keep, hb[0:CONV_HALO, :], 0.0)

    hn_ref[...] = _rms(x_ref[...], g1_ref[...]).astype(BF16)
    z_ref[...] = _dot(hn_ref[...], win_ref[...].astype(BF16)) + bin_ref[...]

    abuf[POOL_HALO:POOL_HALO + tm, :] = z_ref[:, 0:W_A]
    rows = 128
    lane = lax.broadcasted_iota(jnp.int32, (rows, W_A), 1)
    row = lax.broadcasted_iota(jnp.int32, (rows, W_A), 0)
    group = lane // POOL_CH
    window = jnp.full((rows, W_A), POOL_WINDOWS[-1], jnp.int32)
    for gi in range(len(POOL_WINDOWS) - 2, -1, -1):
        window = jnp.where(group == gi, POOL_WINDOWS[gi], window)
    for r0 in range(0, tm, rows):
        s = abuf[r0:r0 + POOL_HALO + rows, :]
        a0 = s[POOL_HALO:]
        win_sum = None
        w = 1
        for gi, target in enumerate(POOL_WINDOWS):
            while w < target:
                s = s + pltpu.roll(s, w, 0)
                w *= 2
            cur = s[POOL_HALO:]
            win_sum = cur if win_sum is None else jnp.where(group >= gi, cur, win_sum)
        pos1 = i * tm + r0 + row + 1
        cnt = jnp.minimum(pos1, window).astype(F32)
        p = (win_sum / cnt - a0).astype(BF16)
        ya = _dot(p, poolw_ref[...]) * pscale_ref[...]
        ycat[r0:r0 + rows, 0:W_A] = ya.astype(BF16)
    abuf[0:POOL_HALO, :] = abuf[tm:tm + POOL_HALO, :]

    wrow = lax.broadcasted_iota(jnp.int32, (SG_HEADS * SG_BLOCK, SG_BLOCK), 0)
    wcol = lax.broadcasted_iota(jnp.int32, (SG_HEADS * SG_BLOCK, SG_BLOCK), 1)
    causal = (wrow & (SG_BLOCK - 1)) >= wcol
    wst = jnp.where(causal, sgw_ref[...], 0.0).astype(BF16)
    hl = lax.broadcasted_iota(jnp.int32, (SG_BLOCK, W_B), 1)
    for blk in range(tm // SG_BLOCK):
        rs = slice(blk * SG_BLOCK, (blk + 1) * SG_BLOCK)
        v = _layer_norm(jax.nn.gelu(z_ref[rs, off_b + W_B:off_b + 2 * W_B]),
                        sglng_ref[...], sglnb_ref[...]).astype(BF16)
        res = _dot(wst, v)
        sv = res[(SG_HEADS - 1) * SG_BLOCK:SG_HEADS * SG_BLOCK]
        for h in range(SG_HEADS - 2, -1, -1):
            sv = jnp.where(hl < (h + 1) * SG_HEAD_DIM, res[h * SG_BLOCK:(h + 1) * SG_BLOCK], sv)
        yb = jax.nn.gelu(z_ref[rs, off_b:off_b + W_B]) * (sv + sgbias_ref[...])
        ycat[rs, W_A:W_A + W_B] = yb.astype(BF16)

    grows = 64
    for r0 in range(0, tm, grows):
        rs = slice(r0, r0 + grows)
        glu = z_ref[rs, off_c:off_c + w_c] * jax.nn.sigmoid(z_ref[rs, off_c + w_c:off_c + 2 * w_c])
        for c in range(n_slab):
            hbufs[c][CONV_HALO + r0:CONV_HALO + r0 + grows, :] = glu[:, c * LANES:(c + 1) * LANES]
    lead = CONV_HALO - (CONV_K - 1)
    crows = 256
    nrow = crows // CONV_ROW_STRIDE
    for c in range(n_slab):
        cs = slice(c * LANES, (c + 1) * LANES)
        for r0 in range(0, tm, crows):
            for ph in range(CONV_ROW_STRIDE):
                acc = None
                for k in range(CONV_K):
                    t = (hbufs[c][pl.ds(r0 + ph + lead + k, nrow, stride=CONV_ROW_STRIDE), :]
                         * convw_ref[k:k + 1, cs])
                    acc = t if acc is None else acc + t
                cbufs[c][pl.ds(r0 + ph, nrow, stride=CONV_ROW_STRIDE), :] = acc + convb_ref[:, cs]
    lrows = 128
    for r0 in range(0, tm, lrows):
        conv = jnp.concatenate([cbufs[c][r0:r0 + lrows, :] for c in range(n_slab)], axis=-1)
        hc = _layer_norm(conv, clng_ref[...], clnb_ref[...])
        ycat[r0:r0 + lrows, W_A + W_B:W_A + W_B + w_c] = jax.nn.silu(hc).astype(BF16)
    for c in range(n_slab):
        hbufs[c][0:CONV_HALO, :] = hbufs[c][tm:tm + CONV_HALO, :]

    y = _dot(ycat[...], wout_ref[...].astype(BF16))
    o_ref[...] = x_ref[...] + _rms(y, g2_ref[...])


def _const_spec(shape, layer):
    nd = len(shape)
    return pl.BlockSpec((None,) + tuple(shape), lambda b, i: (layer,) + (0,) * nd,
                        pipeline_mode=pl.Buffered(1))


def _mixer(x, layer, p):
    bsz, seq, d_model = x.shape
    tm = TM_MIX
    assert seq % tm == 0 and tm % SG_BLOCK == 0
    d_in = p["w_in"].shape[-1]
    w_c = (d_in - W_A - 2 * W_B) // 2
    assert W_A + W_B + w_c == d_model and w_c % LANES == 0
    n_slab = w_c // LANES
    assert CONV_HALO >= CONV_K - 1 and tm % 256 == 0
    assert tm + CONV_HALO <= STRIDED_ROW_LIMIT
    x_spec = pl.BlockSpec((None, tm, d_model), lambda b, i: (b, i, 0))
    cs = functools.partial(_const_spec, layer=layer)
    return pl.pallas_call(
        functools.partial(_mixer_kernel, layout=p["layout"]),
        grid=(bsz, seq // tm),
        in_specs=[
            x_spec,
            cs((1, p["vectors"].shape[-1])),
            cs((d_model, d_in)),
            cs((W_A, W_A)),
            cs((SG_HEADS * SG_BLOCK, SG_BLOCK)),
            cs((SG_BLOCK, W_B)),
            cs((CONV_K, w_c)),
            cs((d_model, d_model)),
        ],
        out_specs=x_spec,
        out_shape=jax.ShapeDtypeStruct(x.shape, x.dtype),
        scratch_shapes=[
            pltpu.VMEM((tm + POOL_HALO, W_A), F32),
            pltpu.VMEM((tm, d_model), BF16),
            pltpu.VMEM((tm, d_model), BF16),
            pltpu.VMEM((tm, d_in), F32),
        ] + [pltpu.VMEM((tm + CONV_HALO, LANES), F32)] * n_slab
          + [pltpu.VMEM((tm, LANES), F32)] * n_slab,
        compiler_params=pltpu.CompilerParams(
            dimension_semantics=("arbitrary", "arbitrary"),
            vmem_limit_bytes=VMEM_LIMIT),
        name=f"mixer_l{layer}",
    )(x, p["vectors"], p["w_in"], p["pool_w"], p["sg_w"], p["sg_b"], p["conv_w"], p["w_out"])


def _attn_kernel(x_ref, vec_ref, wqk_ref, vwo_ref, o_ref, hn_ref, s_ref, p_ref, acc_ref, rinv_ref,
                 *, layout):
    g1_ref, g2_ref = _vector(vec_ref, layout, "pre_x_g"), _vector(vec_ref, layout, "post_x_g")
    tm, d_model = x_ref.shape
    n_mem = d_model // X_HEADS
    hn_ref[...] = (x_ref[...] * g1_ref[...]).astype(BF16)
    s_ref[...] = _dot(hn_ref[...], wqk_ref[...])
    rows = 128
    for r0 in range(0, tm, rows):
        xr = x_ref[r0:r0 + rows, :]
        rinv = lax.rsqrt(jnp.mean(xr * xr, axis=-1, keepdims=True) + EPS)
        rinv_ref[r0:r0 + rows, :] = jnp.broadcast_to(rinv, (rows, LANES))
    for h in range(X_HEADS):
        hs = slice(h * n_mem, (h + 1) * n_mem)
        for r0 in range(0, tm, rows):
            rinv = rinv_ref[r0:r0 + rows, :]
            s = s_ref[r0:r0 + rows, hs] * jnp.concatenate([rinv] * (n_mem // LANES), axis=-1)
            e = jnp.exp(s - jnp.max(s, axis=-1, keepdims=True))
            p_ref[r0:r0 + rows, :] = (e / jnp.sum(e, axis=-1, keepdims=True)).astype(BF16)
        part = _dot(p_ref[...], vwo_ref[hs, :])
        if h == 0:
            acc_ref[...] = part
        else:
            acc_ref[...] += part
    o_ref[...] = x_ref[...] + _rms(acc_ref[...], g2_ref[...])


def _attention(x, layer, p, wqk, vwo):
    bsz, seq, d_model = x.shape
    tm = TM_ATT
    assert seq % tm == 0
    x_spec = pl.BlockSpec((None, tm, d_model), lambda b, i: (b, i, 0))
    kv_spec = pl.BlockSpec((None, None, d_model, d_model), lambda b, i: (layer, b, 0, 0))
    cs = functools.partial(_const_spec, layer=layer)
    return pl.pallas_call(
        functools.partial(_attn_kernel, layout=p["layout"]),
        grid=(bsz, seq // tm),
        in_specs=[x_spec, cs((1, p["vectors"].shape[-1])), kv_spec, kv_spec],
        out_specs=x_spec,
        out_shape=jax.ShapeDtypeStruct(x.shape, x.dtype),
        scratch_shapes=[pltpu.VMEM((tm, d_model), BF16),
                        pltpu.VMEM((tm, d_model), F32),
                        pltpu.VMEM((tm, d_model // X_HEADS), BF16),
                        pltpu.VMEM((tm, d_model), F32),
                        pltpu.VMEM((tm, LANES), F32)],
        compiler_params=pltpu.CompilerParams(
            dimension_semantics=("arbitrary", "arbitrary"),
            vmem_limit_bytes=VMEM_LIMIT),
        name=f"attn_l{layer}",
    )(x, p["vectors"], wqk, vwo)


def _ffn_kernel(x_ref, vec_ref, w1_ref, w2_ref, o_ref, hn_ref, hc_ref, *, layout):
    g1_ref, g2_ref = _vector(vec_ref, layout, "pre_ff_g"), _vector(vec_ref, layout, "post_ff_g")
    d_ff = w1_ref.shape[1]
    hn_ref[...] = _rms(x_ref[...], g1_ref[...]).astype(BF16)
    for c in range(d_ff // FF_CHUNK):
        cs = slice(c * FF_CHUNK, (c + 1) * FF_CHUNK)
        a = jnp.maximum(_dot(hn_ref[...], w1_ref[:, cs].astype(BF16)), 0.0)
        hc_ref[...] = (a * a).astype(BF16)
        part = _dot(hc_ref[...], w2_ref[cs, :].astype(BF16))
        if c == 0:
            o_ref[...] = part
        else:
            o_ref[...] += part
    o_ref[...] = x_ref[...] + _rms(o_ref[...], g2_ref[...])


def _ffn(x, layer, p):
    bsz, seq, d_model = x.shape
    d_ff = p["w_ff1"].shape[-1]
    tm = TM_FF
    assert seq % tm == 0 and d_ff % FF_CHUNK == 0
    x_spec = pl.BlockSpec((None, tm, d_model), lambda b, i: (b, i, 0))
    cs = functools.partial(_const_spec, layer=layer)
    return pl.pallas_call(
        functools.partial(_ffn_kernel, layout=p["layout"]),
        grid=(bsz, seq // tm),
        in_specs=[x_spec, cs((1, p["vectors"].shape[-1])), cs((d_model, d_ff)), cs((d_ff, d_model))],
        out_specs=x_spec,
        out_shape=jax.ShapeDtypeStruct(x.shape, x.dtype),
        scratch_shapes=[pltpu.VMEM((tm, d_model), BF16), pltpu.VMEM((tm, FF_CHUNK), BF16)],
        compiler_params=pltpu.CompilerParams(
            dimension_semantics=("arbitrary", "arbitrary"),
            vmem_limit_bytes=VMEM_LIMIT),
        name=f"ffn_l{layer}",
    )(x, p["vectors"], p["w_ff1"], p["w_ff2"])


def kernel(x, mem, pre_mix_g, w_in, b_in, pool_w, pool_scale, sg_ln_g, sg_ln_b, sg_w, sg_b, conv_w, conv_b, conv_ln_g, conv_ln_b, w_out, post_mix_g, pre_x_g, mem_g, wq, wk, wv, wo, post_x_g, pre_ff_g, w_ff1, w_ff2, post_ff_g):
    n_layers = w_in.shape[0]
    n_groups = pool_w.shape[1]

    vectors = {
        "pre_mix_g": pre_mix_g, "b_in": b_in, "pool_scale": pool_scale,
        "sg_ln_g": sg_ln_g, "sg_ln_b": sg_ln_b, "conv_b": conv_b,
        "conv_ln_g": conv_ln_g, "conv_ln_b": conv_ln_b, "post_mix_g": post_mix_g,
        "pre_x_g": pre_x_g, "post_x_g": post_x_g, "pre_ff_g": pre_ff_g,
        "post_ff_g": post_ff_g, "mem_g": mem_g,
    }
    layout, offset = {}, 0
    for name, v in vectors.items():
        assert v.shape[-1] % LANES == 0
        layout[name] = (offset, v.shape[-1])
        offset += v.shape[-1]
    packed = jnp.concatenate(list(vectors.values()), axis=-1)[:, None, :]
    eye = jnp.eye(n_groups, dtype=pool_w.dtype)
    pool_bd = jnp.einsum("lgcd,gh->lgchd", pool_w, eye).reshape(n_layers, W_A, W_A)
    sg_bias = jnp.repeat(jnp.swapaxes(sg_b, 1, 2), SG_HEAD_DIM, axis=2)
    p = {
        "vectors": packed, "layout": layout, "w_in": w_in,
        "pool_w": pool_bd.astype(BF16),
        "sg_w": sg_w.reshape(n_layers, SG_HEADS * SG_BLOCK, SG_BLOCK), "sg_b": sg_bias,
        "conv_w": conv_w, "w_out": w_out, "w_ff1": w_ff1, "w_ff2": w_ff2,
    }
    wqk, vwo = _fold_memory(mem, packed, layout, wq, wk, wv, wo)
    for layer in range(n_layers):
        x = _mixer(x, layer, p)
        x = _attention(x, layer, p, wqk, vwo)
        x = _ffn(x, layer, p)
    return x
```

```python
import functools
import math

import jax
import jax.numpy as jnp
from jax import lax
from jax.experimental import pallas as pl
from jax.experimental.pallas import tpu as pltpu

EPS = 1e-6
BF16 = jnp.bfloat16
F32 = jnp.float32

POOL_WINDOWS = (2, 4, 8, 16)
POOL_CH = 64
W_A = len(POOL_WINDOWS) * POOL_CH
SG_BLOCK = 128
SG_HEADS = 4
SG_HEAD_DIM = 96
W_B = SG_HEADS * SG_HEAD_DIM
CONV_K = 31
X_HEADS = 4

TM_MIX = 1024
TM_ATT = 512
TM_FF = 1024
LANES = 128
CONV_ROW_STRIDE = 2
STRIDED_ROW_LIMIT = 2048
FF_CHUNK = 1024
POOL_HALO = 16
CONV_HALO = 32
VMEM_LIMIT = 56 * 1024 * 1024


def _dot(a, b):
    return jnp.dot(a, b, preferred_element_type=F32)


def _rms(xf, g):
    ms = jnp.mean(xf * xf, axis=-1, keepdims=True)
    return xf * lax.rsqrt(ms + EPS) * g


def _layer_norm(xf, g, b):
    mu = jnp.mean(xf, axis=-1, keepdims=True)
    xc = xf - mu
    var = jnp.mean(xc * xc, axis=-1, keepdims=True)
    return xc * lax.rsqrt(var + EPS) * g + b


def _vector(vec_ref, layout, name):
    start, size = layout[name]
    return vec_ref.at[:, pl.ds(start, size)]


def _kv_kernel(mem_ref, vec_ref, wq_ref, wk_ref, wv_ref, wo_ref, wqk_ref, vwo_ref, *, layout):
    g_ref = _vector(vec_ref, layout, "mem_g")
    d_model = mem_ref.shape[-1]
    dh = d_model // X_HEADS
    scale = 1.0 / math.sqrt(dh)
    m = _rms(mem_ref[...], g_ref[...]).astype(BF16)
    k = _dot(m, wk_ref[...].astype(BF16)).astype(BF16)
    v = _dot(m, wv_ref[...].astype(BF16)).astype(BF16)
    for h in range(X_HEADS):
        hs = slice(h * dh, (h + 1) * dh)
        s = lax.dot_general(wq_ref[:, hs].astype(BF16), k[:, hs], (((1,), (1,)), ((), ())),
                            preferred_element_type=F32)
        wqk_ref[:, hs] = (s * scale).astype(BF16)
        vwo_ref[hs, :] = _dot(v[:, hs], wo_ref[hs, :].astype(BF16)).astype(BF16)


def _fold_memory(mem, vectors, layout, wq, wk, wv, wo):
    n_layers, d_model, _ = wq.shape
    bsz, n_mem, _ = mem.shape
    assert n_mem * X_HEADS == d_model
    n_vec = vectors.shape[-1]
    w_spec = pl.BlockSpec((None, d_model, d_model), lambda l, b: (l, 0, 0))
    out_spec = pl.BlockSpec((None, None, d_model, d_model), lambda l, b: (l, b, 0, 0))
    out_shape = jax.ShapeDtypeStruct((n_layers, bsz, d_model, d_model), BF16)
    return pl.pallas_call(
        functools.partial(_kv_kernel, layout=layout),
        grid=(n_layers, bsz),
        in_specs=[
            pl.BlockSpec((None, n_mem, d_model), lambda l, b: (b, 0, 0)),
            pl.BlockSpec((None, 1, n_vec), lambda l, b: (l, 0, 0)),
            w_spec, w_spec, w_spec, w_spec,
        ],
        out_specs=[out_spec, out_spec],
        out_shape=[out_shape, out_shape],
        compiler_params=pltpu.CompilerParams(
            dimension_semantics=("arbitrary", "arbitrary"),
            vmem_limit_bytes=VMEM_LIMIT),
        name="fold_memory",
    )(mem, vectors, wq, wk, wv, wo)


def _mixer_kernel(x_ref, vec_ref, win_ref, poolw_ref, sgw_ref, sgbias_ref, convw_ref, wout_ref,
                  o_ref, abuf, ycat, hn_ref, z_ref, *slabs, layout):
    vec = functools.partial(_vector, vec_ref, layout)
    g1_ref, bin_ref, pscale_ref = vec("pre_mix_g"), vec("b_in"), vec("pool_scale")
    sglng_ref, sglnb_ref, convb_ref = vec("sg_ln_g"), vec("sg_ln_b"), vec("conv_b")
    clng_ref, clnb_ref, g2_ref = vec("conv_ln_g"), vec("conv_ln_b"), vec("post_mix_g")
    tm = x_ref.shape[0]
    n_slab = len(slabs) // 2
    hbufs, cbufs = slabs[:n_slab], slabs[n_slab:]
    w_c = n_slab * LANES
    off_b = W_A
    off_c = W_A + 2 * W_B
    i = pl.program_id(1)

    @pl.when(i == 0)
    def _():
        abuf[0:POOL_HALO, :] = jnp.zeros((POOL_HALO, W_A), F32)
        for hb in hbufs:
            hb[0:CONV_HALO, :] = jnp.zeros((CONV_HALO, LANES), F32)

    hn_ref[...] = _rms(x_ref[...], g1_ref[...]).astype(BF16)
    z_ref[...] = _dot(hn_ref[...], win_ref[...].astype(BF16)) + bin_ref[...]

    abuf[POOL_HALO:POOL_HALO + tm, :] = z_ref[:, 0:W_A]
    rows = 128
    lane = lax.broadcasted_iota(jnp.int32, (rows, W_A), 1)
    row = lax.broadcasted_iota(jnp.int32, (rows, W_A), 0)
    group = lane // POOL_CH
    window = jnp.full((rows, W_A), POOL_WINDOWS[-1], jnp.int32)
    for gi in range(len(POOL_WINDOWS) - 2, -1, -1):
        window = jnp.where(group == gi, POOL_WINDOWS[gi], window)
    for r0 in range(0, tm, rows):
        s = abuf[r0:r0 + POOL_HALO + rows, :]
        a0 = s[POOL_HALO:]
        win_sum = None
        w = 1
        for gi, target in enumerate(POOL_WINDOWS):
            while w < target:
                s = s + pltpu.roll(s, w, 0)
                w *= 2
            cur = s[POOL_HALO:]
            win_sum = cur if win_sum is None else jnp.where(group >= gi, cur, win_sum)
        pos1 = i * tm + r0 + row + 1
        cnt = jnp.minimum(pos1, window).astype(F32)
        p = (win_sum / cnt - a0).astype(BF16)
        ya = _dot(p, poolw_ref[...]) * pscale_ref[...]
        ycat[r0:r0 + rows, 0:W_A] = ya.astype(BF16)
    abuf[0:POOL_HALO, :] = abuf[tm:tm + POOL_HALO, :]

    wrow = lax.broadcasted_iota(jnp.int32, (SG_HEADS * SG_BLOCK, SG_BLOCK), 0)
    wcol = lax.broadcasted_iota(jnp.int32, (SG_HEADS * SG_BLOCK, SG_BLOCK), 1)
    causal = (wrow & (SG_BLOCK - 1)) >= wcol
    wst = jnp.where(causal, sgw_ref[...], 0.0).astype(BF16)
    hl = lax.broadcasted_iota(jnp.int32, (SG_BLOCK, W_B), 1)
    for blk in range(tm // SG_BLOCK):
        rs = slice(blk * SG_BLOCK, (blk + 1) * SG_BLOCK)
        v = _layer_norm(jax.nn.gelu(z_ref[rs, off_b + W_B:off_b + 2 * W_B]),
                        sglng_ref[...], sglnb_ref[...]).astype(BF16)
        res = _dot(wst, v)
        sv = res[(SG_HEADS - 1) * SG_BLOCK:SG_HEADS * SG_BLOCK]
        for h in range(SG_HEADS - 2, -1, -1):
            sv = jnp.where(hl < (h + 1) * SG_HEAD_DIM, res[h * SG_BLOCK:(h + 1) * SG_BLOCK], sv)
        yb = jax.nn.gelu(z_ref[rs, off_b:off_b + W_B]) * (sv + sgbias_ref[...])
        ycat[rs, W_A:W_A + W_B] = yb.astype(BF16)

    grows = 64
    for r0 in range(0, tm, grows):
        rs = slice(r0, r0 + grows)
        glu = z_ref[rs, off_c:off_c + w_c] * jax.nn.sigmoid(z_ref[rs, off_c + w_c:off_c + 2 * w_c])
        for c in range(n_slab):
            hbufs[c][CONV_HALO + r0:CONV_HALO + r0 + grows, :] = glu[:, c * LANES:(c + 1) * LANES]
    lead = CONV_HALO - (CONV_K - 1)
    crows = 256
    nrow = crows // CONV_ROW_STRIDE
    for c in range(n_slab):
        cs = slice(c * LANES, (c + 1) * LANES)
        for r0 in range(0, tm, crows):
            for ph in range(CONV_ROW_STRIDE):
                acc = None
                for k in range(CONV_K):
                    t = (hbufs[c][pl.ds(r0 + ph + lead + k, nrow, stride=CONV_ROW_STRIDE), :]
                         * convw_ref[k:k + 1, cs])
                    acc = t if acc is None else acc + t
                cbufs[c][pl.ds(r0 + ph, nrow, stride=CONV_ROW_STRIDE), :] = acc + convb_ref[:, cs]
    lrows = 128
    for r0 in range(0, tm, lrows):
        conv = jnp.concatenate([cbufs[c][r0:r0 + lrows, :] for c in range(n_slab)], axis=-1)
        hc = _layer_norm(conv, clng_ref[...], clnb_ref[...])
        ycat[r0:r0 + lrows, W_A + W_B:W_A + W_B + w_c] = jax.nn.silu(hc).astype(BF16)
    for c in range(n_slab):
        hbufs[c][0:CONV_HALO, :] = hbufs[c][tm:tm + CONV_HALO, :]

    y = _dot(ycat[...], wout_ref[...].astype(BF16))
    o_ref[...] = x_ref[...] + _rms(y, g2_ref[...])


def _const_spec(shape, layer):
    nd = len(shape)
    return pl.BlockSpec((None,) + tuple(shape), lambda b, i: (layer,) + (0,) * nd,
                        pipeline_mode=pl.Buffered(1))


def _mixer(x, layer, p):
    bsz, seq, d_model = x.shape
    tm = TM_MIX
    assert seq % tm == 0 and tm % SG_BLOCK == 0
    d_in = p["w_in"].shape[-1]
    w_c = (d_in - W_A - 2 * W_B) // 2
    assert W_A + W_B + w_c == d_model and w_c % LANES == 0
    n_slab = w_c // LANES
    assert CONV_HALO >= CONV_K - 1 and tm % 256 == 0
    assert tm + CONV_HALO <= STRIDED_ROW_LIMIT
    x_spec = pl.BlockSpec((None, tm, d_model), lambda b, i: (b, i, 0))
    cs = functools.partial(_const_spec, layer=layer)
    return pl.pallas_call(
        functools.partial(_mixer_kernel, layout=p["layout"]),
        grid=(bsz, seq // tm),
        in_specs=[
            x_spec,
            cs((1, p["vectors"].shape[-1])),
            cs((d_model, d_in)),
            cs((W_A, W_A)),
            cs((SG_HEADS * SG_BLOCK, SG_BLOCK)),
            cs((SG_BLOCK, W_B)),
            cs((CONV_K, w_c)),
            cs((d_model, d_model)),
        ],
        out_specs=x_spec,
        out_shape=jax.ShapeDtypeStruct(x.shape, x.dtype),
        scratch_shapes=[
            pltpu.VMEM((tm + POOL_HALO, W_A), F32),
            pltpu.VMEM((tm, d_model), BF16),
            pltpu.VMEM((tm, d_model), BF16),
            pltpu.VMEM((tm, d_in), F32),
        ] + [pltpu.VMEM((tm + CONV_HALO, LANES), F32)] * n_slab
          + [pltpu.VMEM((tm, LANES), F32)] * n_slab,
        compiler_params=pltpu.CompilerParams(
            dimension_semantics=("arbitrary", "arbitrary"),
            vmem_limit_bytes=VMEM_LIMIT),
        input_output_aliases={0: 0} if layer > 0 else {},
        name=f"mixer_l{layer}",
    )(x, p["vectors"], p["w_in"], p["pool_w"], p["sg_w"], p["sg_b"], p["conv_w"], p["w_out"])


def _attn_kernel(x_ref, vec_ref, wqk_ref, vwo_ref, o_ref, hn_ref, s_ref, p_ref, acc_ref, rinv_ref,
                 *, layout):
    g1_ref, g2_ref = _vector(vec_ref, layout, "pre_x_g"), _vector(vec_ref, layout, "post_x_g")
    tm, d_model = x_ref.shape
    n_mem = d_model // X_HEADS
    hn_ref[...] = (x_ref[...] * g1_ref[...]).astype(BF16)
    s_ref[...] = _dot(hn_ref[...], wqk_ref[...])
    rows = 128
    for r0 in range(0, tm, rows):
        xr = x_ref[r0:r0 + rows, :]
        rinv = lax.rsqrt(jnp.mean(xr * xr, axis=-1, keepdims=True) + EPS)
        rinv_ref[r0:r0 + rows, :] = jnp.broadcast_to(rinv, (rows, LANES))
    for h in range(X_HEADS):
        hs = slice(h * n_mem, (h + 1) * n_mem)
        for r0 in range(0, tm, rows):
            rinv = rinv_ref[r0:r0 + rows, :]
            s = s_ref[r0:r0 + rows, hs] * jnp.concatenate([rinv] * (n_mem // LANES), axis=-1)
            e = jnp.exp(s - jnp.max(s, axis=-1, keepdims=True))
            p_ref[r0:r0 + rows, :] = (e / jnp.sum(e, axis=-1, keepdims=True)).astype(BF16)
        part = _dot(p_ref[...], vwo_ref[hs, :])
        if h == 0:
            acc_ref[...] = part
        else:
            acc_ref[...] += part
    o_ref[...] = x_ref[...] + _rms(acc_ref[...], g2_ref[...])


def _attention(x, layer, p, wqk, vwo):
    bsz, seq, d_model = x.shape
    tm = TM_ATT
    assert seq % tm == 0
    x_spec = pl.BlockSpec((None, tm, d_model), lambda b, i: (b, i, 0))
    kv_spec = pl.BlockSpec((None, None, d_model, d_model), lambda b, i: (layer, b, 0, 0))
    cs = functools.partial(_const_spec, layer=layer)
    return pl.pallas_call(
        functools.partial(_attn_kernel, layout=p["layout"]),
        grid=(bsz, seq // tm),
        in_specs=[x_spec, cs((1, p["vectors"].shape[-1])), kv_spec, kv_spec],
        out_specs=x_spec,
        out_shape=jax.ShapeDtypeStruct(x.shape, x.dtype),
        scratch_shapes=[pltpu.VMEM((tm, d_model), BF16),
                        pltpu.VMEM((tm, d_model), F32),
                        pltpu.VMEM((tm, d_model // X_HEADS), BF16),
                        pltpu.VMEM((tm, d_model), F32),
                        pltpu.VMEM((tm, LANES), F32)],
        compiler_params=pltpu.CompilerParams(
            dimension_semantics=("arbitrary", "arbitrary"),
            vmem_limit_bytes=VMEM_LIMIT),
        input_output_aliases={0: 0},
        name=f"attn_l{layer}",
    )(x, p["vectors"], wqk, vwo)


def _ffn_kernel(x_ref, vec_ref, w1_ref, w2_ref, o_ref, hn_ref, hc_ref, *, layout):
    g1_ref, g2_ref = _vector(vec_ref, layout, "pre_ff_g"), _vector(vec_ref, layout, "post_ff_g")
    d_ff = w1_ref.shape[1]
    hn_ref[...] = _rms(x_ref[...], g1_ref[...]).astype(BF16)
    for c in range(d_ff // FF_CHUNK):
        cs = slice(c * FF_CHUNK, (c + 1) * FF_CHUNK)
        a = jnp.maximum(_dot(hn_ref[...], w1_ref[:, cs].astype(BF16)), 0.0)
        hc_ref[...] = (a * a).astype(BF16)
        part = _dot(hc_ref[...], w2_ref[cs, :].astype(BF16))
        if c == 0:
            o_ref[...] = part
        else:
            o_ref[...] += part
    o_ref[...] = x_ref[...] + _rms(o_ref[...], g2_ref[...])


def _ffn(x, layer, p):
    bsz, seq, d_model = x.shape
    d_ff = p["w_ff1"].shape[-1]
    tm = TM_FF
    assert seq % tm == 0 and d_ff % FF_CHUNK == 0
    x_spec = pl.BlockSpec((None, tm, d_model), lambda b, i: (b, i, 0))
    cs = functools.partial(_const_spec, layer=layer)
    return pl.pallas_call(
        functools.partial(_ffn_kernel, layout=p["layout"]),
        grid=(bsz, seq // tm),
        in_specs=[x_spec, cs((1, p["vectors"].shape[-1])), cs((d_model, d_ff)), cs((d_ff, d_model))],
        out_specs=x_spec,
        out_shape=jax.ShapeDtypeStruct(x.shape, x.dtype),
        scratch_shapes=[pltpu.VMEM((tm, d_model), BF16), pltpu.VMEM((tm, FF_CHUNK), BF16)],
        compiler_params=pltpu.CompilerParams(
            dimension_semantics=("arbitrary", "arbitrary"),
            vmem_limit_bytes=VMEM_LIMIT),
        input_output_aliases={0: 0},
        name=f"ffn_l{layer}",
    )(x, p["vectors"], p["w_ff1"], p["w_ff2"])


def kernel(x, mem, pre_mix_g, w_in, b_in, pool_w, pool_scale, sg_ln_g, sg_ln_b, sg_w, sg_b, conv_w, conv_b, conv_ln_g, conv_ln_b, w_out, post_mix_g, pre_x_g, mem_g, wq, wk, wv, wo, post_x_g, pre_ff_g, w_ff1, w_ff2, post_ff_g):
    n_layers = w_in.shape[0]
    n_groups = pool_w.shape[1]

    vectors = {
        "pre_mix_g": pre_mix_g, "b_in": b_in, "pool_scale": pool_scale,
        "sg_ln_g": sg_ln_g, "sg_ln_b": sg_ln_b, "conv_b": conv_b,
        "conv_ln_g": conv_ln_g, "conv_ln_b": conv_ln_b, "post_mix_g": post_mix_g,
        "pre_x_g": pre_x_g, "post_x_g": post_x_g, "pre_ff_g": pre_ff_g,
        "post_ff_g": post_ff_g, "mem_g": mem_g,
    }
    layout, offset = {}, 0
    for name, v in vectors.items():
        assert v.shape[-1] % LANES == 0
        layout[name] = (offset, v.shape[-1])
        offset += v.shape[-1]
    packed = jnp.concatenate(list(vectors.values()), axis=-1)[:, None, :]
    eye = jnp.eye(n_groups, dtype=pool_w.dtype)
    pool_bd = jnp.einsum("lgcd,gh->lgchd", pool_w, eye).reshape(n_layers, W_A, W_A)
    sg_bias = jnp.repeat(jnp.swapaxes(sg_b, 1, 2), SG_HEAD_DIM, axis=2)
    p = {
        "vectors": packed, "layout": layout, "w_in": w_in,
        "pool_w": pool_bd.astype(BF16),
        "sg_w": sg_w.reshape(n_layers, SG_HEADS * SG_BLOCK, SG_BLOCK), "sg_b": sg_bias,
        "conv_w": conv_w, "w_out": w_out, "w_ff1": w_ff1, "w_ff2": w_ff2,
    }
    wqk, vwo = _fold_memory(mem, packed, layout, wq, wk, wv, wo)
    for layer in range(n_layers):
        x = _mixer(x, layer, p)
        x = _attention(x, layer, p, wqk, vwo)
        x = _ffn(x, layer, p)
    return x
```
